```python
import jax, jax.numpy as jnp
from jax import lax
import numpy as np

D_MODEL = 1024
BATCH = 8
SEQ = 4096
DEPTH = 1

D_MIX = D_MODEL
D_CONV = D_MIX // 2
N_CONV_HEADS = 8
CONV_WIDTH = 31
D_POOL = D_MIX - D_CONV
POOL_WINDOWS = (2, 4, 8, 16)
N_POOL_GROUPS = len(POOL_WINDOWS)
POOL_GROUP_DIM = D_POOL // N_POOL_GROUPS
D_IN = 2 * D_CONV + D_POOL
D_FF = 2816
FFN_CONV_WIDTH = 3
EPS = 1e-6

kernel_name = "hybrid_conformer_pool_convffn_block"


def _rmsnorm(x, g):
    xf = x.astype(jnp.float32)
    y = xf * lax.rsqrt(jnp.mean(xf * xf, axis=-1, keepdims=True) + EPS)
    return (y * g.astype(jnp.float32)).astype(x.dtype)


def _layernorm(x, g, b):
    xf = x.astype(jnp.float32)
    mu = jnp.mean(xf, axis=-1, keepdims=True)
    xc = xf - mu
    var = jnp.mean(xc * xc, axis=-1, keepdims=True)
    y = xc * lax.rsqrt(var + EPS) * g.astype(jnp.float32) + b.astype(jnp.float32)
    return y.astype(x.dtype)


def _causal_dwconv(x, w, b):
    k, c = w.shape
    y = lax.conv_general_dilated(
        x, w[:, None, :].astype(x.dtype), window_strides=(1,),
        padding=((k - 1, 0),), dimension_numbers=("NWC", "WIO", "NWC"),
        feature_group_count=c)
    return y + b


def _conformer_conv_mixer(a_val, a_gate, conv_w, conv_b, ln_g, ln_b):
    h = a_val * jax.nn.sigmoid(a_gate)
    h = _causal_dwconv(h, conv_w, conv_b)
    h = _layernorm(h, ln_g, ln_b)
    return jax.nn.silu(h)


def _multiscale_pool_mixer(xb, pool_w, pool_scale):
    bsz, s, _ = xb.shape
    xg = xb.reshape(bsz, s, N_POOL_GROUPS, POOL_GROUP_DIM).astype(jnp.float32)
    cs = jnp.cumsum(xg, axis=1)
    pos = jnp.arange(1, s + 1, dtype=jnp.float32)
    outs = []
    for gi, w in enumerate(POOL_WINDOWS):
        c = cs[:, :, gi]
        lag = jnp.pad(c[:, : s - w], ((0, 0), (w, 0), (0, 0)))
        cnt = jnp.minimum(pos, float(w))[None, :, None]
        outs.append((c - lag) / cnt - xg[:, :, gi])
    d = jnp.stack(outs, axis=2).astype(xb.dtype)
    y = jnp.einsum("bsgc,gcd->bsgd", d, pool_w).reshape(bsz, s, D_POOL)
    return y * pool_scale


def setup_inputs(seed: int = 0) -> dict:
    key = jax.random.key(seed)
    ks = jax.random.split(key, 17)
    f32 = jnp.float32
    L = DEPTH

    def nrm(k, shape, scale):
        return jax.random.normal(k, shape, f32) * scale

    return {
        "x": jax.random.normal(ks[0], (BATCH, SEQ, D_MODEL), f32),
        "norm_mix_g": 1.0 + nrm(ks[1], (L, D_MODEL), 0.05),
        "w_in": nrm(ks[2], (L, D_MODEL, D_IN), D_MODEL ** -0.5),
        "conv_a_w": nrm(ks[3], (L, CONV_WIDTH, D_CONV), CONV_WIDTH ** -0.5),
        "conv_a_b": nrm(ks[4], (L, D_CONV), 0.02),
        "ln_a_g": 1.0 + nrm(ks[5], (L, D_CONV), 0.05),
        "ln_a_b": nrm(ks[6], (L, D_CONV), 0.02),
        "pool_w": nrm(ks[7], (L, N_POOL_GROUPS, POOL_GROUP_DIM, POOL_GROUP_DIM), POOL_GROUP_DIM ** -0.5),
        "pool_scale": 1.0 + nrm(ks[8], (L, D_POOL), 0.1),
        "w_out": nrm(ks[9], (L, D_MIX, D_MODEL), D_MIX ** -0.5),
        "norm_ffn_g": 1.0 + nrm(ks[10], (L, D_MODEL), 0.05),
        "w_up": nrm(ks[11], (L, D_MODEL, 2 * D_FF), D_MODEL ** -0.5),
        "conv_f_w": nrm(ks[12], (L, FFN_CONV_WIDTH, D_FF), FFN_CONV_WIDTH ** -0.5),
        "conv_f_b": nrm(ks[13], (L, D_FF), 0.02),
        "w_down": nrm(ks[14], (L, D_FF, D_MODEL), D_FF ** -0.5),
        "norm_final_g": 1.0 + nrm(ks[15], (D_MODEL,), 0.05),
    }


def reference(x, norm_mix_g, w_in, conv_a_w, conv_a_b, ln_a_g, ln_a_b, pool_w,
              pool_scale, w_out, norm_ffn_g, w_up, conv_f_w, conv_f_b, w_down,
              norm_final_g):
    for l in range(DEPTH):
        h = _rmsnorm(x, norm_mix_g[l])
        proj = h @ w_in[l]
        a_val, a_gate, b_in = jnp.split(proj, [D_CONV, 2 * D_CONV], axis=-1)
        ya = _conformer_conv_mixer(a_val, a_gate, conv_a_w[l], conv_a_b[l],
                                   ln_a_g[l], ln_a_b[l])
        yb = _multiscale_pool_mixer(b_in, pool_w[l], pool_scale[l])
        x = x + jnp.concatenate([ya, yb], axis=-1) @ w_out[l]
        h = _rmsnorm(x, norm_ffn_g[l])
        gate, val = jnp.split(h @ w_up[l], 2, axis=-1)
        gate = _causal_dwconv(gate, conv_f_w[l], conv_f_b[l])
        x = x + (jax.nn.silu(gate) * val) @ w_down[l]
    return _rmsnorm(x, norm_final_g)
```

```python
import functools

import jax
import jax.numpy as jnp
from jax import lax
from jax.experimental import pallas as pl
from jax.experimental.pallas import tpu as pltpu

D_MODEL = 1024
D_CONV = 512
D_POOL = 512
CONV_WIDTH = 31
POOL_WINDOWS = (2, 4, 8, 16)
POOL_GROUP_DIM = D_POOL // len(POOL_WINDOWS)
D_IN = 2 * D_CONV + D_POOL
D_FF = 2816
FFN_CONV_WIDTH = 3
EPS = 1e-6

SUBLANES = 8
TILE = 256
FF_CHUNK = 256
N_FF_CHUNKS = D_FF // FF_CHUNK
CONV_ROWS = 64
HALO_A = 32
HALO_B = 16
HALO_F = SUBLANES
VMEM_LIMIT_BYTES = 56 * 1024 * 1024


def _rms_scale(v):
    return v * lax.rsqrt(jnp.mean(v * v, axis=-1, keepdims=True) + EPS)


def _block_kernel(x_ref, gmix_ref, win_ref, cw_ref, cb_ref, lng_ref, lnb_ref,
                  pw_ref, ps_ref, wout_ref, gffn_ref, wup_ref, fw_ref, fb_ref,
                  wdown_ref, gfin_ref, o_ref, uext_ref, bext_ref, gext_ref, mix_ref):
    j = pl.program_id(1)

    @pl.when(j == 0)
    def _():
        uext_ref[0:HALO_A, :] = jnp.zeros((HALO_A, D_CONV), jnp.float32)
        bext_ref[0:HALO_B, :] = jnp.zeros((HALO_B, D_POOL), jnp.float32)
        gext_ref[0:HALO_F, :] = jnp.zeros((HALO_F, D_FF), jnp.float32)

    x = x_ref[...]
    h = (_rms_scale(x) * gmix_ref[...]).astype(jnp.bfloat16)
    proj = jnp.dot(h, win_ref[...], preferred_element_type=jnp.float32)

    a_val = proj[:, 0:D_CONV]
    a_gate = proj[:, D_CONV:2 * D_CONV]
    uext_ref[HALO_A:HALO_A + TILE, :] = a_val * jax.nn.sigmoid(a_gate)
    bext_ref[HALO_B:HALO_B + TILE, :] = proj[:, 2 * D_CONV:D_IN]

    for r0 in range(0, TILE, CONV_ROWS):
        acc = jnp.broadcast_to(cb_ref[...], (CONV_ROWS, D_CONV))
        for k in range(CONV_WIDTH):
            start = HALO_A + r0 - (CONV_WIDTH - 1) + k
            acc = acc + cw_ref[k:k + 1, :] * uext_ref[start:start + CONV_ROWS, :]
        mu = jnp.mean(acc, axis=-1, keepdims=True)
        xc = acc - mu
        var = jnp.mean(xc * xc, axis=-1, keepdims=True)
        y = xc * lax.rsqrt(var + EPS) * lng_ref[...] + lnb_ref[...]
        mix_ref[r0:r0 + CONV_ROWS, 0:D_CONV] = (y * jax.nn.sigmoid(y)).astype(jnp.bfloat16)

    pos = (j * TILE + 1 + lax.broadcasted_iota(jnp.int32, (TILE, 1), 0)).astype(jnp.float32)
    for gi, w in enumerate(POOL_WINDOWS):
        c0 = gi * POOL_GROUP_DIM
        tok = bext_ref[HALO_B:HALO_B + TILE, c0:c0 + POOL_GROUP_DIM]
        wsum = tok
        for i in range(1, w):
            wsum = wsum + bext_ref[HALO_B - i:HALO_B - i + TILE, c0:c0 + POOL_GROUP_DIM]
        d = wsum / jnp.minimum(pos, float(w)) - tok
        yb = jnp.dot(d.astype(jnp.bfloat16), pw_ref[gi], preferred_element_type=jnp.float32)
        yb = yb * ps_ref[:, c0:c0 + POOL_GROUP_DIM]
        mix_ref[:, D_CONV + c0:D_CONV + c0 + POOL_GROUP_DIM] = yb.astype(jnp.bfloat16)

    uext_ref[0:HALO_A, :] = uext_ref[TILE:TILE + HALO_A, :]
    bext_ref[0:HALO_B, :] = bext_ref[TILE:TILE + HALO_B, :]

    x_mid = x + jnp.dot(mix_ref[...], wout_ref[...], preferred_element_type=jnp.float32)

    h2 = (_rms_scale(x_mid) * gffn_ref[...]).astype(jnp.bfloat16)
    ffn = jnp.zeros((TILE, D_MODEL), jnp.float32)
    for c in range(N_FF_CHUNKS):
        f0 = c * FF_CHUNK
        up = jnp.dot(h2, wup_ref[:, 2 * f0:2 * f0 + 2 * FF_CHUNK],
                     preferred_element_type=jnp.float32)
        gate = up[:, 0:FF_CHUNK]
        val = up[:, FF_CHUNK:2 * FF_CHUNK]
        gext_ref[HALO_F:HALO_F + TILE, f0:f0 + FF_CHUNK] = gate
        conv = fb_ref[:, f0:f0 + FF_CHUNK] + fw_ref[2:3, f0:f0 + FF_CHUNK] * gate
        for k in range(FFN_CONV_WIDTH - 1):
            start = HALO_F - (FFN_CONV_WIDTH - 1) + k
            conv = conv + fw_ref[k:k + 1, f0:f0 + FF_CHUNK] * gext_ref[start:start + TILE, f0:f0 + FF_CHUNK]
        gext_ref[0:HALO_F, f0:f0 + FF_CHUNK] = gext_ref[TILE:TILE + HALO_F, f0:f0 + FF_CHUNK]
        act = (conv * jax.nn.sigmoid(conv) * val).astype(jnp.bfloat16)
        ffn = ffn + jnp.dot(act, wdown_ref[f0:f0 + FF_CHUNK, :], preferred_element_type=jnp.float32)

    o_ref[...] = _rms_scale(x_mid + ffn) * gfin_ref[...]


def _resident(shape):
    zeros = (0,) * len(shape)
    return pl.BlockSpec(shape, lambda b, j: zeros, pipeline_mode=pl.Buffered(1))


@jax.jit
def kernel(x, norm_mix_g, w_in, conv_a_w, conv_a_b, ln_a_g, ln_a_b, pool_w, pool_scale, w_out, norm_ffn_g, w_up, conv_f_w, conv_f_b, w_down, norm_final_g):
    batch, seq, d_model = x.shape
    assert d_model == D_MODEL and seq % TILE == 0
    assert w_in.shape[0] == 1, "one layer"
    bf16 = jnp.bfloat16

    w_up_c = (w_up[0].reshape(D_MODEL, 2, N_FF_CHUNKS, FF_CHUNK)
              .transpose(0, 2, 1, 3).reshape(D_MODEL, 2 * D_FF).astype(bf16))
    row = lambda v: v.reshape(1, -1)

    operands = (
        x, row(norm_mix_g[0]), w_in[0].astype(bf16), conv_a_w[0], row(conv_a_b[0]),
        row(ln_a_g[0]), row(ln_a_b[0]), pool_w[0].astype(bf16), row(pool_scale[0]),
        w_out[0].astype(bf16), row(norm_ffn_g[0]), w_up_c, conv_f_w[0], row(conv_f_b[0]),
        w_down[0].astype(bf16), row(norm_final_g),
    )
    x_spec = pl.BlockSpec((None, TILE, D_MODEL), lambda b, j: (b, j, 0))
    in_specs = [x_spec] + [_resident(op.shape) for op in operands[1:]]

    return pl.pallas_call(
        _block_kernel,
        out_shape=jax.ShapeDtypeStruct(x.shape, x.dtype),
        grid=(batch, seq // TILE),
        in_specs=in_specs,
        out_specs=x_spec,
        scratch_shapes=[
            pltpu.VMEM((HALO_A + TILE, D_CONV), jnp.float32),
            pltpu.VMEM((HALO_B + TILE, D_POOL), jnp.float32),
            pltpu.VMEM((HALO_F + TILE, D_FF), jnp.float32),
            pltpu.VMEM((TILE, D_MODEL), jnp.bfloat16),
        ],
        compiler_params=pltpu.CompilerParams(
            dimension_semantics=("arbitrary", "arbitrary"),
            vmem_limit_bytes=VMEM_LIMIT_BYTES,
        ),
        name="hybrid_block",
    )(*operands)
```

```python
import functools

import jax
import jax.numpy as jnp
from jax import lax
from jax.experimental import pallas as pl
from jax.experimental.pallas import tpu as pltpu

D_MODEL = 1024
D_CONV = 512
D_POOL = 512
CONV_WIDTH = 31
POOL_WINDOWS = (2, 4, 8, 16)
POOL_GROUP_DIM = D_POOL // len(POOL_WINDOWS)
D_IN = 2 * D_CONV + D_POOL
D_FF = 2816
FFN_CONV_WIDTH = 3
EPS = 1e-6

BATCH = 8
TT = 32
ROWS = TT * BATCH
FF_CHUNK = 256
N_FF_CHUNKS = D_FF // FF_CHUNK
UP_LOOKAHEAD = 2
DOWN_GROUP = 4
N_DOWN_GROUPS = -(-N_FF_CHUNKS // DOWN_GROUP)
CONV_ROWS = 32
N_CONV_CHUNKS = ROWS // CONV_ROWS
HIST_A = (CONV_WIDTH - 1) * BATCH
HIST_B = (max(POOL_WINDOWS) - 1) * BATCH
HIST_F = (FFN_CONV_WIDTH - 1) * BATCH
VMEM_LIMIT_BYTES = 56 * 1024 * 1024


def _rms_scale(v):
    return v * lax.rsqrt(jnp.mean(v * v, axis=-1, keepdims=True) + EPS)


def _deal(pieces, n_slots):
    base, extra = divmod(len(pieces), n_slots)
    out, i = [], 0
    for s in range(n_slots):
        n = base + (1 if s < extra else 0)
        out.append(pieces[i:i + n])
        i += n
    return out


def _block_kernel(n_tiles,
                  x_hbm, gmix_ref, win_ref, cw_ref, cb_ref, lng_ref, lnb_ref,
                  pw_ref, ps_ref, wout_ref, gffn_ref, wgate_ref, wval_ref, fw_ref, fb_ref,
                  wdown_ref, gfin_ref, o_hbm,
                  xin_ref, obuf_ref, h_ref, h2_ref, uext_ref, bext_ref, gext_ref, val_ref,
                  act_ref, mix_ref, xmid_ref, ffn_ref, sem_in, sem_out):
    g = pl.program_id(0)
    slot = lax.rem(g, 2)
    prev = 1 - slot

    def in_copy(tile, s, b):
        return pltpu.make_async_copy(x_hbm.at[b, pl.ds(tile * TT, TT), :],
                                     xin_ref.at[s, :, b, :], sem_in.at[s, b])

    def out_copy(tile, s, b):
        return pltpu.make_async_copy(obuf_ref.at[s, :, b, :],
                                     o_hbm.at[b, pl.ds(tile * TT, TT), :], sem_out.at[s, b])

    @pl.when(g == 0)
    def _():
        for b in range(BATCH):
            in_copy(0, 0, b).start()
        uext_ref[0:HIST_A, :] = jnp.zeros((HIST_A, D_CONV), jnp.float32)
        bext_ref[0:HIST_B, :] = jnp.zeros((HIST_B, D_POOL), jnp.float32)
        gext_ref[:, 0:HIST_F, :] = jnp.zeros((N_FF_CHUNKS, HIST_F, FF_CHUNK), jnp.float32)
        xmid_ref[1] = jnp.zeros((ROWS, D_MODEL), jnp.float32)

    @pl.when(g + 1 < n_tiles)
    def _():
        for b in range(BATCH):
            in_copy(g + 1, prev, b).start()

    @pl.when(g < n_tiles)
    def _():
        for b in range(BATCH):
            in_copy(g, slot, b).wait()

    @pl.when(g >= 3)
    def _():
        for b in range(BATCH):
            out_copy(g - 3, prev, b).wait()

    def ffn_up(c):
        cols = slice(c * FF_CHUNK, (c + 1) * FF_CHUNK)
        gext_ref[c, HIST_F:HIST_F + ROWS, :] = jnp.dot(
            h2_ref[...], wgate_ref[:, cols], preferred_element_type=jnp.float32)
        val_ref[c] = jnp.dot(
            h2_ref[...], wval_ref[:, cols], preferred_element_type=jnp.float32)

    def ffn_act(c):
        cols = slice(c * FF_CHUNK, (c + 1) * FF_CHUNK)
        conv = fb_ref[:, cols] + fw_ref[FFN_CONV_WIDTH - 1:FFN_CONV_WIDTH, cols] * gext_ref[c, HIST_F:HIST_F + ROWS, :]
        for k in range(FFN_CONV_WIDTH - 1):
            conv = conv + fw_ref[k:k + 1, cols] * gext_ref[c, k * BATCH:k * BATCH + ROWS, :]
        gext_ref[c, 0:HIST_F, :] = gext_ref[c, ROWS:ROWS + HIST_F, :]
        grp, sub = divmod(c, DOWN_GROUP)
        act_ref[grp, :, sub * FF_CHUNK:(sub + 1) * FF_CHUNK] = (
            conv * jax.nn.sigmoid(conv) * val_ref[c]).astype(jnp.bfloat16)

    def ffn_down(c_lo, c_hi):
        ks = slice(c_lo * FF_CHUNK, c_hi * FF_CHUNK)
        part = jnp.dot(act_ref[c_lo // DOWN_GROUP, :, 0:(c_hi - c_lo) * FF_CHUNK], wdown_ref[ks, :],
                       preferred_element_type=jnp.float32)
        if c_hi == N_FF_CHUNKS:
            total = part if c_lo == 0 else ffn_ref[...] + part
            out = _rms_scale(xmid_ref[prev] + total) * gfin_ref[...]
            obuf_ref[prev] = out.reshape(TT, BATCH, D_MODEL)
        elif c_lo == 0:
            ffn_ref[...] = part
        else:
            ffn_ref[...] += part

    def x_tile():
        return xin_ref[slot].reshape(ROWS, D_MODEL)

    def mixer_head():
        h_ref[...] = (_rms_scale(x_tile()) * gmix_ref[...]).astype(jnp.bfloat16)
        proj = jnp.dot(h_ref[...], win_ref[:, 0:2 * D_CONV], preferred_element_type=jnp.float32)
        uext_ref[HIST_A:HIST_A + ROWS, :] = proj[:, 0:D_CONV] * jax.nn.sigmoid(proj[:, D_CONV:2 * D_CONV])

    def conv_chunk(ci):
        r0 = ci * CONV_ROWS
        steps = CONV_ROWS // BATCH
        acc = jnp.broadcast_to(cb_ref[...], (CONV_ROWS, D_CONV)).reshape(steps, BATCH, D_CONV)
        for k in range(CONV_WIDTH):
            start = r0 + k * BATCH
            tap = uext_ref[start:start + CONV_ROWS, :].reshape(steps, BATCH, D_CONV)
            acc = acc + cw_ref[k] * tap
        acc = acc.reshape(CONV_ROWS, D_CONV)
        mu = jnp.mean(acc, axis=-1, keepdims=True)
        xc = acc - mu
        var = jnp.mean(xc * xc, axis=-1, keepdims=True)
        y = xc * lax.rsqrt(var + EPS) * lng_ref[...] + lnb_ref[...]
        mix_ref[r0:r0 + CONV_ROWS, 0:D_CONV] = (y * jax.nn.sigmoid(y)).astype(jnp.bfloat16)

    def pool_in():
        bext_ref[HIST_B:HIST_B + ROWS, :] = jnp.dot(
            h_ref[...], win_ref[:, 2 * D_CONV:D_IN], preferred_element_type=jnp.float32)

    def pool_group(gi):
        w = POOL_WINDOWS[gi]
        c0 = gi * POOL_GROUP_DIM
        step = lax.broadcasted_iota(jnp.int32, (ROWS, 1), 0) // BATCH
        pos = (g * TT + 1 + step).astype(jnp.float32)
        tok = bext_ref[HIST_B:HIST_B + ROWS, c0:c0 + POOL_GROUP_DIM]
        wsum = tok
        for i in range(1, w):
            lo = HIST_B - i * BATCH
            wsum = wsum + bext_ref[lo:lo + ROWS, c0:c0 + POOL_GROUP_DIM]
        d = wsum / jnp.minimum(pos, float(w)) - tok
        yb = jnp.dot(d.astype(jnp.bfloat16), pw_ref[gi], preferred_element_type=jnp.float32)
        yb = yb * ps_ref[:, c0:c0 + POOL_GROUP_DIM]
        mix_ref[:, D_CONV + c0:D_CONV + c0 + POOL_GROUP_DIM] = yb.astype(jnp.bfloat16)

    def mixer_tail():
        uext_ref[0:HIST_A, :] = uext_ref[ROWS:ROWS + HIST_A, :]
        bext_ref[0:HIST_B, :] = bext_ref[ROWS:ROWS + HIST_B, :]
        xmid_ref[slot] = x_tile() + jnp.dot(mix_ref[...], wout_ref[...],
                                            preferred_element_type=jnp.float32)

    mixer_pieces = ([mixer_head]
                    + [functools.partial(conv_chunk, ci) for ci in range(N_CONV_CHUNKS)]
                    + [pool_in]
                    + [functools.partial(pool_group, gi) for gi in range(len(POOL_WINDOWS))]
                    + [mixer_tail])
    mixer_slots = _deal(mixer_pieces, N_FF_CHUNKS - 1) + [[]]

    h2_ref[...] = (_rms_scale(xmid_ref[prev]) * gffn_ref[...]).astype(jnp.bfloat16)
    for c in range(min(UP_LOOKAHEAD, N_FF_CHUNKS)):
        ffn_up(c)
    for c in range(N_FF_CHUNKS):
        if c + UP_LOOKAHEAD < N_FF_CHUNKS:
            ffn_up(c + UP_LOOKAHEAD)
        for piece in mixer_slots[c]:
            piece()
        ffn_act(c)
        if (c + 1) % DOWN_GROUP == 0 or c + 1 == N_FF_CHUNKS:
            ffn_down(c // DOWN_GROUP * DOWN_GROUP, c + 1)

    @pl.when(g >= 1)
    def _():
        for b in range(BATCH):
            out_copy(g - 1, prev, b).start()

    @pl.when(g == n_tiles)
    def _():
        for b in range(BATCH):
            out_copy(g - 2, slot, b).wait()
        for b in range(BATCH):
            out_copy(g - 1, prev, b).wait()


def _resident(shape):
    zeros = (0,) * len(shape)
    return pl.BlockSpec(shape, lambda g: zeros, pipeline_mode=pl.Buffered(1))


@jax.jit
def kernel(x, norm_mix_g, w_in, conv_a_w, conv_a_b, ln_a_g, ln_a_b, pool_w, pool_scale, w_out, norm_ffn_g, w_up, conv_f_w, conv_f_b, w_down, norm_final_g):
    batch, seq, d_model = x.shape
    assert batch == BATCH and d_model == D_MODEL and seq % TT == 0
    assert w_in.shape[0] == 1, "one layer"
    bf16 = jnp.bfloat16
    f32 = jnp.float32
    n_tiles = seq // TT
    assert n_tiles >= 2
    row = lambda v: v.reshape(1, -1)

    operands = (
        x, row(norm_mix_g[0]), w_in[0].astype(bf16),
        jnp.broadcast_to(conv_a_w[0][:, None, :], (CONV_WIDTH, BATCH, D_CONV)),
        row(conv_a_b[0]), row(ln_a_g[0]), row(ln_a_b[0]), pool_w[0].astype(bf16), row(pool_scale[0]),
        w_out[0].astype(bf16), row(norm_ffn_g[0]),
        w_up[0, :, :D_FF].astype(bf16), w_up[0, :, D_FF:].astype(bf16),
        conv_f_w[0], row(conv_f_b[0]), w_down[0].astype(bf16), row(norm_final_g),
    )
    in_specs = ([pl.BlockSpec(memory_space=pl.ANY)]
                + [_resident(op.shape) for op in operands[1:]])

    return pl.pallas_call(
        functools.partial(_block_kernel, n_tiles),
        out_shape=jax.ShapeDtypeStruct(x.shape, x.dtype),
        grid=(n_tiles + 1,),
        in_specs=in_specs,
        out_specs=pl.BlockSpec(memory_space=pl.ANY),
        scratch_shapes=[
            pltpu.VMEM((2, TT, BATCH, D_MODEL), f32),
            pltpu.VMEM((2, TT, BATCH, D_MODEL), f32),
            pltpu.VMEM((ROWS, D_MODEL), bf16),
            pltpu.VMEM((ROWS, D_MODEL), bf16),
            pltpu.VMEM((HIST_A + ROWS, D_CONV), f32),
            pltpu.VMEM((HIST_B + ROWS, D_POOL), f32),
            pltpu.VMEM((N_FF_CHUNKS, HIST_F + ROWS, FF_CHUNK), f32),
            pltpu.VMEM((N_FF_CHUNKS, ROWS, FF_CHUNK), f32),
            pltpu.VMEM((N_DOWN_GROUPS, ROWS, DOWN_GROUP * FF_CHUNK), bf16),
            pltpu.VMEM((ROWS, D_MODEL), bf16),
            pltpu.VMEM((2, ROWS, D_MODEL), f32),
            pltpu.VMEM((ROWS, D_MODEL), f32),
            pltpu.SemaphoreType.DMA((2, BATCH)),
            pltpu.SemaphoreType.DMA((2, BATCH)),
        ],
        compiler_params=pltpu.CompilerParams(
            dimension_semantics=("arbitrary",),
            vmem_limit_bytes=VMEM_LIMIT_BYTES,
        ),
        name="hybrid_block",
    )(*operands)
```

```python
import functools

import jax
import jax.numpy as jnp
from jax import lax
from jax.experimental import pallas as pl
from jax.experimental.pallas import tpu as pltpu

D_MODEL = 1024
D_CONV = 512
D_POOL = 512
CONV_WIDTH = 31
POOL_WINDOWS = (2, 4, 8, 16)
POOL_GROUP_DIM = D_POOL // len(POOL_WINDOWS)
D_IN = 2 * D_CONV + D_POOL
D_FF = 2816
FFN_CONV_WIDTH = 3
EPS = 1e-6

BATCH = 8
LANES = 128
TT = 32
ROWS = TT * BATCH
FF_CHUNK = 256
N_FF_CHUNKS = D_FF // FF_CHUNK
UP_LOOKAHEAD = 2
DOWN_GROUP = 4
N_DOWN_GROUPS = -(-N_FF_CHUNKS // DOWN_GROUP)
CONV_STEPS = 8
CONV_TAPS = 8
CONV_ROWS = CONV_STEPS * BATCH
N_CONV_CHUNKS = ROWS // CONV_ROWS
HIST_A = (CONV_WIDTH - 1) * BATCH
HIST_B = (max(POOL_WINDOWS) - 1) * BATCH
HIST_F = (FFN_CONV_WIDTH - 1) * BATCH
CONV_OUT0 = HIST_A + ROWS
CONV_W0 = CONV_OUT0 + ROWS
FFN0 = CONV_W0 + CONV_WIDTH * BATCH
FFN_BLOCK_ROWS = HIST_F + ROWS
WORK_ROWS = FFN0 + N_FF_CHUNKS * FFN_BLOCK_ROWS
VMEM_LIMIT_BYTES = 56 * 1024 * 1024


def _rms_scale(v):
    return v * lax.rsqrt(jnp.mean(v * v, axis=-1, keepdims=True) + EPS)


def _deal(pieces, n_slots):
    base, extra = divmod(len(pieces), n_slots)
    out, i = [], 0
    for s in range(n_slots):
        n = base + (1 if s < extra else 0)
        out.append(pieces[i:i + n])
        i += n
    return out


def _block_kernel(n_tiles,
                  zero_ref, x_hbm, gmix_ref, win_ref, cw_ref, cb_ref, lng_ref, lnb_ref,
                  pw_ref, ps_ref, wout_ref, gffn_ref, wgate_ref, wval_ref, fw_ref, fb_ref,
                  wdown_ref, gfin_ref, o_hbm,
                  xin_ref, obuf_ref, h_ref, h2_ref, work_ref, bext_ref,
                  act_ref, mix_ref, xmid_ref, ffn_ref, sem_in, sem_out):
    g = pl.program_id(0)
    slot = lax.rem(g, 2)
    prev = 1 - slot

    def in_copy(tile, s, b):
        return pltpu.make_async_copy(x_hbm.at[b, pl.ds(tile * TT, TT), :],
                                     xin_ref.at[s, :, b, :], sem_in.at[s, b])

    def out_copy(tile, s, b):
        return pltpu.make_async_copy(obuf_ref.at[s, :, b, :],
                                     o_hbm.at[b, pl.ds(tile * TT, TT), :], sem_out.at[s, b])

    def gate_rows(c, start, size):
        lo = FFN0 + c * FFN_BLOCK_ROWS + start
        return slice(lo, lo + size)

    @pl.when(g == 0)
    def _():
        for b in range(BATCH):
            in_copy(0, 0, b).start()
        work_ref[0:HIST_A, :] = jnp.zeros((HIST_A, D_CONV), jnp.float32)
        for k in range(CONV_WIDTH):
            work_ref[CONV_W0 + k * BATCH:CONV_W0 + (k + 1) * BATCH, :] = cw_ref[k]
        bext_ref[0:HIST_B, :] = jnp.zeros((HIST_B, D_POOL), jnp.float32)
        for c in range(N_FF_CHUNKS):
            work_ref[gate_rows(c, 0, HIST_F), 0:FF_CHUNK] = jnp.zeros((HIST_F, FF_CHUNK), jnp.float32)
        xmid_ref[1] = jnp.zeros((ROWS, D_MODEL), jnp.float32)

    @pl.when(g + 1 < n_tiles)
    def _():
        for b in range(BATCH):
            in_copy(g + 1, prev, b).start()

    @pl.when(g < n_tiles)
    def _():
        for b in range(BATCH):
            in_copy(g, slot, b).wait()

    @pl.when(g >= 3)
    def _():
        for b in range(BATCH):
            out_copy(g - 3, prev, b).wait()

    def ffn_up(c):
        cols = slice(c * FF_CHUNK, (c + 1) * FF_CHUNK)
        work_ref[gate_rows(c, HIST_F, ROWS), 0:FF_CHUNK] = jnp.dot(
            h2_ref[...], wgate_ref[:, cols], preferred_element_type=jnp.float32)
        work_ref[gate_rows(c, HIST_F, ROWS), FF_CHUNK:2 * FF_CHUNK] = jnp.dot(
            h2_ref[...], wval_ref[:, cols], preferred_element_type=jnp.float32)

    def ffn_act(c):
        cols = slice(c * FF_CHUNK, (c + 1) * FF_CHUNK)
        gate = lambda k: work_ref[gate_rows(c, k * BATCH, ROWS), 0:FF_CHUNK]
        conv = fb_ref[:, cols] + fw_ref[FFN_CONV_WIDTH - 1:FFN_CONV_WIDTH, cols] * gate(FFN_CONV_WIDTH - 1)
        for k in range(FFN_CONV_WIDTH - 1):
            conv = conv + fw_ref[k:k + 1, cols] * gate(k)
        work_ref[gate_rows(c, 0, HIST_F), 0:FF_CHUNK] = work_ref[gate_rows(c, ROWS, HIST_F), 0:FF_CHUNK]
        grp, sub = divmod(c, DOWN_GROUP)
        act_ref[grp, :, sub * FF_CHUNK:(sub + 1) * FF_CHUNK] = (
            conv * jax.nn.sigmoid(conv)
            * work_ref[gate_rows(c, HIST_F, ROWS), FF_CHUNK:2 * FF_CHUNK]).astype(jnp.bfloat16)

    def ffn_down(c_lo, c_hi):
        ks = slice(c_lo * FF_CHUNK, c_hi * FF_CHUNK)
        part = jnp.dot(act_ref[c_lo // DOWN_GROUP, :, 0:(c_hi - c_lo) * FF_CHUNK], wdown_ref[ks, :],
                       preferred_element_type=jnp.float32)
        if c_hi == N_FF_CHUNKS:
            total = part if c_lo == 0 else ffn_ref[...] + part
            out = _rms_scale(xmid_ref[prev] + total) * gfin_ref[...]
            obuf_ref[prev] = out.reshape(TT, BATCH, D_MODEL)
        elif c_lo == 0:
            ffn_ref[...] = part
        else:
            ffn_ref[...] += part

    def x_tile():
        return xin_ref[slot].reshape(ROWS, D_MODEL)

    def mixer_head():
        h_ref[...] = (_rms_scale(x_tile()) * gmix_ref[...]).astype(jnp.bfloat16)
        proj = jnp.dot(h_ref[...], win_ref[:, 0:2 * D_CONV], preferred_element_type=jnp.float32)
        work_ref[HIST_A:HIST_A + ROWS, :] = proj[:, 0:D_CONV] * jax.nn.sigmoid(proj[:, D_CONV:2 * D_CONV])

    row_zero = pl.multiple_of(zero_ref[0], BATCH)

    def conv_rows(start):
        return pl.ds(row_zero + start, BATCH)

    def conv_block(ci, lc):
        r0 = ci * CONV_ROWS
        lanes = slice(lc * LANES, (lc + 1) * LANES)
        acc = [jnp.broadcast_to(cb_ref[:, lanes], (BATCH, LANES))] * CONV_STEPS
        for k0 in range(0, CONV_WIDTH, CONV_TAPS):
            taps = range(k0, min(k0 + CONV_TAPS, CONV_WIDTH))
            w = {k: work_ref[conv_rows(CONV_W0 + k * BATCH), lanes] for k in taps}
            u = {j: work_ref[conv_rows(r0 + j * BATCH), lanes]
                 for j in range(taps[0], taps[-1] + CONV_STEPS)}
            for k in taps:
                for t in range(CONV_STEPS):
                    acc[t] = acc[t] + w[k] * u[t + k]
        for t in range(CONV_STEPS):
            out0 = CONV_OUT0 + r0 + t * BATCH
            work_ref[out0:out0 + BATCH, lanes] = acc[t]

    def conv_norm(ci):
        r0 = ci * CONV_ROWS
        acc = work_ref[CONV_OUT0 + r0:CONV_OUT0 + r0 + CONV_ROWS, :]
        mu = jnp.mean(acc, axis=-1, keepdims=True)
        xc = acc - mu
        var = jnp.mean(xc * xc, axis=-1, keepdims=True)
        y = xc * lax.rsqrt(var + EPS) * lng_ref[...] + lnb_ref[...]
        mix_ref[r0:r0 + CONV_ROWS, 0:D_CONV] = (y * jax.nn.sigmoid(y)).astype(jnp.bfloat16)

    def pool_in():
        bext_ref[HIST_B:HIST_B + ROWS, :] = jnp.dot(
            h_ref[...], win_ref[:, 2 * D_CONV:D_IN], preferred_element_type=jnp.float32)

    def pool_group(gi):
        w = POOL_WINDOWS[gi]
        c0 = gi * POOL_GROUP_DIM
        step = lax.broadcasted_iota(jnp.int32, (ROWS, 1), 0) // BATCH
        pos = (g * TT + 1 + step).astype(jnp.float32)
        tok = bext_ref[HIST_B:HIST_B + ROWS, c0:c0 + POOL_GROUP_DIM]
        wsum = tok
        for i in range(1, w):
            lo = HIST_B - i * BATCH
            wsum = wsum + bext_ref[lo:lo + ROWS, c0:c0 + POOL_GROUP_DIM]
        d = wsum / jnp.minimum(pos, float(w)) - tok
        yb = jnp.dot(d.astype(jnp.bfloat16), pw_ref[gi], preferred_element_type=jnp.float32)
        yb = yb * ps_ref[:, c0:c0 + POOL_GROUP_DIM]
        mix_ref[:, D_CONV + c0:D_CONV + c0 + POOL_GROUP_DIM] = yb.astype(jnp.bfloat16)

    def mixer_tail():
        work_ref[0:HIST_A, :] = work_ref[ROWS:ROWS + HIST_A, :]
        bext_ref[0:HIST_B, :] = bext_ref[ROWS:ROWS + HIST_B, :]
        xmid_ref[slot] = x_tile() + jnp.dot(mix_ref[...], wout_ref[...],
                                            preferred_element_type=jnp.float32)

    conv_pieces = [functools.partial(piece, ci)
                   for ci in range(N_CONV_CHUNKS)
                   for piece in ([functools.partial(conv_block, lc=lc) for lc in range(D_CONV // LANES)]
                                 + [conv_norm])]
    mixer_pieces = ([mixer_head] + conv_pieces + [pool_in]
                    + [functools.partial(pool_group, gi) for gi in range(len(POOL_WINDOWS))]
                    + [mixer_tail])
    mixer_slots = _deal(mixer_pieces, N_FF_CHUNKS - 1) + [[]]

    h2_ref[...] = (_rms_scale(xmid_ref[prev]) * gffn_ref[...]).astype(jnp.bfloat16)
    for c in range(min(UP_LOOKAHEAD, N_FF_CHUNKS)):
        ffn_up(c)
    for c in range(N_FF_CHUNKS):
        if c + UP_LOOKAHEAD < N_FF_CHUNKS:
            ffn_up(c + UP_LOOKAHEAD)
        for piece in mixer_slots[c]:
            piece()
        ffn_act(c)
        if (c + 1) % DOWN_GROUP == 0 or c + 1 == N_FF_CHUNKS:
            ffn_down(c // DOWN_GROUP * DOWN_GROUP, c + 1)

    @pl.when(g >= 1)
    def _():
        for b in range(BATCH):
            out_copy(g - 1, prev, b).start()

    @pl.when(g == n_tiles)
    def _():
        for b in range(BATCH):
            out_copy(g - 2, slot, b).wait()
        for b in range(BATCH):
            out_copy(g - 1, prev, b).wait()


def _resident(shape):
    zeros = (0,) * len(shape)
    return pl.BlockSpec(shape, lambda g: zeros, pipeline_mode=pl.Buffered(1))


@jax.jit
def kernel(x, norm_mix_g, w_in, conv_a_w, conv_a_b, ln_a_g, ln_a_b, pool_w, pool_scale, w_out, norm_ffn_g, w_up, conv_f_w, conv_f_b, w_down, norm_final_g):
    batch, seq, d_model = x.shape
    assert batch == BATCH and d_model == D_MODEL and seq % TT == 0
    assert w_in.shape[0] == 1, "one layer"
    bf16 = jnp.bfloat16
    f32 = jnp.float32
    n_tiles = seq // TT
    assert n_tiles >= 2
    row = lambda v: v.reshape(1, -1)

    operands = (
        jnp.zeros((1,), jnp.int32), x, row(norm_mix_g[0]), w_in[0].astype(bf16),
        jnp.broadcast_to(conv_a_w[0][:, None, :], (CONV_WIDTH, BATCH, D_CONV)),
        row(conv_a_b[0]), row(ln_a_g[0]), row(ln_a_b[0]), pool_w[0].astype(bf16), row(pool_scale[0]),
        w_out[0].astype(bf16), row(norm_ffn_g[0]),
        w_up[0, :, :D_FF].astype(bf16), w_up[0, :, D_FF:].astype(bf16),
        conv_f_w[0], row(conv_f_b[0]), w_down[0].astype(bf16), row(norm_final_g),
    )
    in_specs = ([pl.BlockSpec(memory_space=pltpu.SMEM), pl.BlockSpec(memory_space=pl.ANY)]
                + [_resident(op.shape) for op in operands[2:]])

    return pl.pallas_call(
        functools.partial(_block_kernel, n_tiles),
        out_shape=jax.ShapeDtypeStruct(x.shape, x.dtype),
        grid=(n_tiles + 1,),
        in_specs=in_specs,
        out_specs=pl.BlockSpec(memory_space=pl.ANY),
        scratch_shapes=[
            pltpu.VMEM((2, TT, BATCH, D_MODEL), f32),
            pltpu.VMEM((2, TT, BATCH, D_MODEL), f32),
            pltpu.VMEM((ROWS, D_MODEL), bf16),
            pltpu.VMEM((ROWS, D_MODEL), bf16),
            pltpu.VMEM((WORK_ROWS, D_CONV), f32),
            pltpu.VMEM((HIST_B + ROWS, D_POOL), f32),
            pltpu.VMEM((N_DOWN_GROUPS, ROWS, DOWN_GROUP * FF_CHUNK), bf16),
            pltpu.VMEM((ROWS, D_MODEL), bf16),
            pltpu.VMEM((2, ROWS, D_MODEL), f32),
            pltpu.VMEM((ROWS, D_MODEL), f32),
            pltpu.SemaphoreType.DMA((2, BATCH)),
            pltpu.SemaphoreType.DMA((2, BATCH)),
        ],
        compiler_params=pltpu.CompilerParams(
            dimension_semantics=("arbitrary",),
            vmem_limit_bytes=VMEM_LIMIT_BYTES,
        ),
        name="hybrid_block",
    )(*operands)
```

```python
import functools

import jax
import jax.numpy as jnp
from jax import lax
from jax.experimental import pallas as pl
from jax.experimental.pallas import tpu as pltpu

D_MODEL = 1024
D_CONV = 512
D_POOL = 512
CONV_WIDTH = 31
POOL_WINDOWS = (2, 4, 8, 16)
POOL_GROUP_DIM = D_POOL // len(POOL_WINDOWS)
D_IN = 2 * D_CONV + D_POOL
D_FF = 2816
FFN_CONV_WIDTH = 3
EPS = 1e-6

BATCH = 8
LANES = 128
TT = 32
ROWS = TT * BATCH
N_XIN = 3
FF_CHUNK = 256
N_FF_CHUNKS = D_FF // FF_CHUNK
UP_LOOKAHEAD = 2
DOWN_GROUP = 4
N_DOWN_GROUPS = -(-N_FF_CHUNKS // DOWN_GROUP)
CONV_STEPS = 8
CONV_TAPS = 8
CONV_ROWS = CONV_STEPS * BATCH
N_CONV_CHUNKS = ROWS // CONV_ROWS
HIST_A = (CONV_WIDTH - 1) * BATCH
HIST_B = (max(POOL_WINDOWS) - 1) * BATCH
HIST_F = (FFN_CONV_WIDTH - 1) * BATCH
CONV_OUT0 = HIST_A + ROWS
CONV_W0 = CONV_OUT0 + ROWS
FFN0 = CONV_W0 + CONV_WIDTH * BATCH
FFN_BLOCK_ROWS = HIST_F + ROWS
WORK_ROWS = FFN0 + N_FF_CHUNKS * FFN_BLOCK_ROWS
VMEM_LIMIT_BYTES = 56 * 1024 * 1024


def _rms_scale(v):
    return v * lax.rsqrt(jnp.mean(v * v, axis=-1, keepdims=True) + EPS)


def _deal(pieces, n_slots):
    base, extra = divmod(len(pieces), n_slots)
    out, i = [], 0
    for s in range(n_slots):
        n = base + (1 if s < extra else 0)
        out.append(pieces[i:i + n])
        i += n
    return out


def _block_kernel(n_tiles,
                  zero_ref, x_hbm, gmix_ref, win_ref, cw_ref, cb_ref, lng_ref, lnb_ref,
                  pw_ref, ps_ref, wout_ref, gffn_ref, wup_ref, fw_ref, fb_ref,
                  wdown_ref, gfin_ref, o_hbm,
                  xin_ref, obuf_ref, h_ref, h2_ref, work_ref, bext_ref,
                  act_ref, mix_ref, xmid_ref, ffn_ref, sem_in, sem_out):
    g = pl.program_id(0)
    slot = lax.rem(g, 2)
    prev = 1 - slot
    xslot = lax.rem(g, N_XIN)

    def in_copy(tile, s, b):
        return pltpu.make_async_copy(x_hbm.at[b, pl.ds(tile * TT, TT), :],
                                     xin_ref.at[s, :, b, :], sem_in.at[s, b])

    def out_copy(tile, s, b):
        return pltpu.make_async_copy(obuf_ref.at[s, :, b, :],
                                     o_hbm.at[b, pl.ds(tile * TT, TT), :], sem_out.at[s, b])

    def x_tile(s):
        return xin_ref[s].reshape(ROWS, D_MODEL)

    def mixer_norm(s):
        h_ref[...] = (_rms_scale(x_tile(s)) * gmix_ref[...]).astype(jnp.bfloat16)

    def gate_rows(c, start, size):
        lo = FFN0 + c * FFN_BLOCK_ROWS + start
        return slice(lo, lo + size)

    @pl.when(g == 0)
    def _():
        for t in range(2):
            for b in range(BATCH):
                in_copy(t, t, b).start()
        for b in range(BATCH):
            in_copy(0, 0, b).wait()
        mixer_norm(0)
        work_ref[0:HIST_A, :] = jnp.zeros((HIST_A, D_CONV), jnp.float32)
        for k in range(CONV_WIDTH):
            work_ref[CONV_W0 + k * BATCH:CONV_W0 + (k + 1) * BATCH, :] = cw_ref[k]
        bext_ref[0:HIST_B, :] = jnp.zeros((HIST_B, D_POOL), jnp.float32)
        for c in range(N_FF_CHUNKS):
            work_ref[gate_rows(c, 0, HIST_F), 0:FF_CHUNK] = jnp.zeros((HIST_F, FF_CHUNK), jnp.float32)
        xmid_ref[1] = jnp.zeros((ROWS, D_MODEL), jnp.float32)

    @pl.when(g + 2 < n_tiles)
    def _():
        for b in range(BATCH):
            in_copy(g + 2, lax.rem(g + 2, N_XIN), b).start()

    @pl.when(g + 1 < n_tiles)
    def _():
        for b in range(BATCH):
            in_copy(g + 1, lax.rem(g + 1, N_XIN), b).wait()

    @pl.when(g >= 3)
    def _():
        for b in range(BATCH):
            out_copy(g - 3, prev, b).wait()

    def ffn_up(c):
        cols = slice(c * FF_CHUNK, (c + 1) * FF_CHUNK)
        work_ref[gate_rows(c, HIST_F, ROWS), 0:FF_CHUNK] = jnp.dot(
            h2_ref[...], wup_ref[:, cols], preferred_element_type=jnp.float32)
        work_ref[gate_rows(c, HIST_F, ROWS), FF_CHUNK:2 * FF_CHUNK] = jnp.dot(
            h2_ref[...], wup_ref[:, D_FF + c * FF_CHUNK:D_FF + (c + 1) * FF_CHUNK],
            preferred_element_type=jnp.float32)

    def ffn_act(c):
        cols = slice(c * FF_CHUNK, (c + 1) * FF_CHUNK)
        gate = lambda k: work_ref[gate_rows(c, k * BATCH, ROWS), 0:FF_CHUNK]
        conv = fb_ref[:, cols] + fw_ref[FFN_CONV_WIDTH - 1:FFN_CONV_WIDTH, cols] * gate(FFN_CONV_WIDTH - 1)
        for k in range(FFN_CONV_WIDTH - 1):
            conv = conv + fw_ref[k:k + 1, cols] * gate(k)
        work_ref[gate_rows(c, 0, HIST_F), 0:FF_CHUNK] = work_ref[gate_rows(c, ROWS, HIST_F), 0:FF_CHUNK]
        grp, sub = divmod(c, DOWN_GROUP)
        act_ref[grp, :, sub * FF_CHUNK:(sub + 1) * FF_CHUNK] = (
            conv * jax.nn.sigmoid(conv)
            * work_ref[gate_rows(c, HIST_F, ROWS), FF_CHUNK:2 * FF_CHUNK]).astype(jnp.bfloat16)

    def ffn_down(c_lo, c_hi):
        ks = slice(c_lo * FF_CHUNK, c_hi * FF_CHUNK)
        part = jnp.dot(act_ref[c_lo // DOWN_GROUP, :, 0:(c_hi - c_lo) * FF_CHUNK], wdown_ref[ks, :],
                       preferred_element_type=jnp.float32)
        if c_hi == N_FF_CHUNKS:
            total = part if c_lo == 0 else ffn_ref[...] + part
            out = _rms_scale(xmid_ref[prev] + total) * gfin_ref[...]
            obuf_ref[prev] = out.reshape(TT, BATCH, D_MODEL)
        elif c_lo == 0:
            ffn_ref[...] = part
        else:
            ffn_ref[...] += part

    def mixer_head():
        proj = jnp.dot(h_ref[...], win_ref[:, 0:2 * D_CONV], preferred_element_type=jnp.float32)
        work_ref[HIST_A:HIST_A + ROWS, :] = proj[:, 0:D_CONV] * jax.nn.sigmoid(proj[:, D_CONV:2 * D_CONV])

    row_zero = pl.multiple_of(zero_ref[0], BATCH)

    def conv_rows(start):
        return pl.ds(row_zero + start, BATCH)

    def conv_block(ci, lc):
        r0 = ci * CONV_ROWS
        lanes = slice(lc * LANES, (lc + 1) * LANES)
        acc = [jnp.broadcast_to(cb_ref[:, lanes], (BATCH, LANES))] * CONV_STEPS
        for k0 in range(0, CONV_WIDTH, CONV_TAPS):
            taps = range(k0, min(k0 + CONV_TAPS, CONV_WIDTH))
            w = {k: work_ref[conv_rows(CONV_W0 + k * BATCH), lanes] for k in taps}
            u = {j: work_ref[conv_rows(r0 + j * BATCH), lanes]
                 for j in range(taps[0], taps[-1] + CONV_STEPS)}
            for k in taps:
                for t in range(CONV_STEPS):
                    acc[t] = acc[t] + w[k] * u[t + k]
        for t in range(CONV_STEPS):
            out0 = CONV_OUT0 + r0 + t * BATCH
            work_ref[out0:out0 + BATCH, lanes] = acc[t]

    def conv_norm(ci):
        r0 = ci * CONV_ROWS
        acc = work_ref[CONV_OUT0 + r0:CONV_OUT0 + r0 + CONV_ROWS, :]
        mu = jnp.mean(acc, axis=-1, keepdims=True)
        xc = acc - mu
        var = jnp.mean(xc * xc, axis=-1, keepdims=True)
        y = xc * lax.rsqrt(var + EPS) * lng_ref[...] + lnb_ref[...]
        mix_ref[r0:r0 + CONV_ROWS, 0:D_CONV] = (y * jax.nn.sigmoid(y)).astype(jnp.bfloat16)

    def pool_in():
        bext_ref[HIST_B:HIST_B + ROWS, :] = jnp.dot(
            h_ref[...], win_ref[:, 2 * D_CONV:D_IN], preferred_element_type=jnp.float32)

    def pool_group(gi):
        w = POOL_WINDOWS[gi]
        c0 = gi * POOL_GROUP_DIM
        step = lax.broadcasted_iota(jnp.int32, (ROWS, 1), 0) // BATCH
        pos = (g * TT + 1 + step).astype(jnp.float32)
        tok = bext_ref[HIST_B:HIST_B + ROWS, c0:c0 + POOL_GROUP_DIM]
        wsum = tok
        for i in range(1, w):
            lo = HIST_B - i * BATCH
            wsum = wsum + bext_ref[lo:lo + ROWS, c0:c0 + POOL_GROUP_DIM]
        d = wsum / jnp.minimum(pos, float(w)) - tok
        yb = jnp.dot(d.astype(jnp.bfloat16), pw_ref[gi], preferred_element_type=jnp.float32)
        yb = yb * ps_ref[:, c0:c0 + POOL_GROUP_DIM]
        mix_ref[:, D_CONV + c0:D_CONV + c0 + POOL_GROUP_DIM] = yb.astype(jnp.bfloat16)

    def mixer_tail():
        work_ref[0:HIST_A, :] = work_ref[ROWS:ROWS + HIST_A, :]
        bext_ref[0:HIST_B, :] = bext_ref[ROWS:ROWS + HIST_B, :]
        xmid_ref[slot] = x_tile(xslot) + jnp.dot(mix_ref[...], wout_ref[...],
                                                 preferred_element_type=jnp.float32)

    conv_pieces = [functools.partial(piece, ci)
                   for ci in range(N_CONV_CHUNKS)
                   for piece in ([functools.partial(conv_block, lc=lc) for lc in range(D_CONV // LANES)]
                                 + [conv_norm])]
    mixer_pieces = (conv_pieces + [pool_in]
                    + [functools.partial(pool_group, gi) for gi in range(len(POOL_WINDOWS))])
    mixer_slots = _deal(mixer_pieces, N_FF_CHUNKS - 1) + [[]]

    mixer_head()
    h2_ref[...] = (_rms_scale(xmid_ref[prev]) * gffn_ref[...]).astype(jnp.bfloat16)
    for c in range(min(UP_LOOKAHEAD, N_FF_CHUNKS)):
        ffn_up(c)
    for c in range(N_FF_CHUNKS):
        for piece in mixer_slots[c]:
            piece()
        if c + UP_LOOKAHEAD < N_FF_CHUNKS:
            ffn_up(c + UP_LOOKAHEAD)
        ffn_act(c)
        if (c + 1) % DOWN_GROUP == 0 or c + 1 == N_FF_CHUNKS:
            ffn_down(c // DOWN_GROUP * DOWN_GROUP, c + 1)
    mixer_norm(lax.rem(g + 1, N_XIN))
    mixer_tail()

    @pl.when(g >= 1)
    def _():
        for b in range(BATCH):
            out_copy(g - 1, prev, b).start()

    @pl.when(g == n_tiles)
    def _():
        for b in range(BATCH):
            out_copy(g - 2, slot, b).wait()
        for b in range(BATCH):
            out_copy(g - 1, prev, b).wait()


def _resident(shape):
    zeros = (0,) * len(shape)
    return pl.BlockSpec(shape, lambda g: zeros, pipeline_mode=pl.Buffered(1))


@jax.jit
def kernel(x, norm_mix_g, w_in, conv_a_w, conv_a_b, ln_a_g, ln_a_b, pool_w, pool_scale, w_out, norm_ffn_g, w_up, conv_f_w, conv_f_b, w_down, norm_final_g):
    batch, seq, d_model = x.shape
    assert batch == BATCH and d_model == D_MODEL and seq % TT == 0
    assert w_in.shape[0] == 1, "one layer"
    bf16 = jnp.bfloat16
    f32 = jnp.float32
    n_tiles = seq // TT
    assert n_tiles >= N_XIN
    row = lambda v: v.reshape(1, -1)

    operands = (
        jnp.zeros((1,), jnp.int32), x, row(norm_mix_g[0]), w_in[0].astype(bf16),
        jnp.broadcast_to(conv_a_w[0][:, None, :], (CONV_WIDTH, BATCH, D_CONV)),
        row(conv_a_b[0]), row(ln_a_g[0]), row(ln_a_b[0]), pool_w[0].astype(bf16), row(pool_scale[0]),
        w_out[0].astype(bf16), row(norm_ffn_g[0]),
        w_up[0].astype(bf16),
        conv_f_w[0], row(conv_f_b[0]), w_down[0].astype(bf16), row(norm_final_g),
    )
    in_specs = ([pl.BlockSpec(memory_space=pltpu.SMEM), pl.BlockSpec(memory_space=pl.ANY)]
                + [_resident(op.shape) for op in operands[2:]])

    return pl.pallas_call(
        functools.partial(_block_kernel, n_tiles),
        out_shape=jax.ShapeDtypeStruct(x.shape, x.dtype),
        grid=(n_tiles + 1,),
        in_specs=in_specs,
        out_specs=pl.BlockSpec(memory_space=pl.ANY),
        scratch_shapes=[
            pltpu.VMEM((N_XIN, TT, BATCH, D_MODEL), f32),
            pltpu.VMEM((2, TT, BATCH, D_MODEL), f32),
            pltpu.VMEM((ROWS, D_MODEL), bf16),
            pltpu.VMEM((ROWS, D_MODEL), bf16),
            pltpu.VMEM((WORK_ROWS, D_CONV), f32),
            pltpu.VMEM((HIST_B + ROWS, D_POOL), f32),
            pltpu.VMEM((N_DOWN_GROUPS, ROWS, DOWN_GROUP * FF_CHUNK), bf16),
            pltpu.VMEM((ROWS, D_MODEL), bf16),
            pltpu.VMEM((2, ROWS, D_MODEL), f32),
            pltpu.VMEM((ROWS, D_MODEL), f32),
            pltpu.SemaphoreType.DMA((N_XIN, BATCH)),
            pltpu.SemaphoreType.DMA((2, BATCH)),
        ],
        compiler_params=pltpu.CompilerParams(
            dimension_semantics=("arbitrary",),
            vmem_limit_bytes=VMEM_LIMIT_BYTES,
        ),
        name="hybrid_block",
    )(*operands)
```

```python
import functools

import jax
import jax.numpy as jnp
from jax import lax
from jax.experimental import pallas as pl
from jax.experimental.pallas import tpu as pltpu

D_MODEL = 1024
D_CONV = 512
D_POOL = 512
CONV_WIDTH = 31
POOL_WINDOWS = (2, 4, 8, 16)
POOL_GROUP_DIM = D_POOL // len(POOL_WINDOWS)
D_IN = 2 * D_CONV + D_POOL
D_FF = 2816
FFN_CONV_WIDTH = 3
EPS = 1e-6

BATCH = 8
LANES = 128
TT = 32
ROWS = TT * BATCH
N_XIN = 3
FF_CHUNK = 256
N_FF_CHUNKS = D_FF // FF_CHUNK
UP_LOOKAHEAD = 2
DOWN_GROUP = 11
N_DOWN_GROUPS = -(-N_FF_CHUNKS // DOWN_GROUP)
CONV_STEPS = 8
CONV_TAPS = 8
CONV_ROWS = CONV_STEPS * BATCH
N_CONV_CHUNKS = ROWS // CONV_ROWS
HIST_A = (CONV_WIDTH - 1) * BATCH
HIST_B = (max(POOL_WINDOWS) - 1) * BATCH
HIST_F = (FFN_CONV_WIDTH - 1) * BATCH
CONV_OUT0 = HIST_A + ROWS
CONV_W0 = CONV_OUT0 + ROWS
FFN0 = CONV_W0 + CONV_WIDTH * BATCH
FFN_BLOCK_ROWS = HIST_F + ROWS
WORK_ROWS = FFN0 + N_FF_CHUNKS * FFN_BLOCK_ROWS
VMEM_LIMIT_BYTES = 56 * 1024 * 1024


def _rms_scale(v):
    return v * lax.rsqrt(jnp.mean(v * v, axis=-1, keepdims=True) + EPS)


def _deal(pieces, n_slots):
    base, extra = divmod(len(pieces), n_slots)
    out, i = [], 0
    for s in range(n_slots):
        n = base + (1 if s < extra else 0)
        out.append(pieces[i:i + n])
        i += n
    return out


def _block_kernel(n_tiles,
                  zero_ref, x_hbm, gmix_ref, win_ref, cw_ref, cb_ref, lng_ref, lnb_ref,
                  pw_ref, ps_ref, wout_ref, gffn_ref, wup_ref, fw_ref, fb_ref,
                  wdown_ref, gfin_ref, o_hbm,
                  xin_ref, obuf_ref, h_ref, h2_ref, work_ref, bext_ref,
                  act_ref, mix_ref, xmid_ref, ffn_ref, sem_in, sem_out):
    g = pl.program_id(0)
    slot = lax.rem(g, 2)
    prev = 1 - slot
    xslot = lax.rem(g, N_XIN)

    def in_copy(tile, s, b):
        return pltpu.make_async_copy(x_hbm.at[b, pl.ds(tile * TT, TT), :],
                                     xin_ref.at[s, :, b, :], sem_in.at[s, b])

    def out_copy(tile, s, b):
        return pltpu.make_async_copy(obuf_ref.at[s, :, b, :],
                                     o_hbm.at[b, pl.ds(tile * TT, TT), :], sem_out.at[s, b])

    def x_tile(s):
        return xin_ref[s].reshape(ROWS, D_MODEL)

    def mixer_norm(s):
        h_ref[...] = (_rms_scale(x_tile(s)) * gmix_ref[...]).astype(jnp.bfloat16)

    def gate_rows(c, start, size):
        lo = FFN0 + c * FFN_BLOCK_ROWS + start
        return slice(lo, lo + size)

    @pl.when(g == 0)
    def _():
        for t in range(2):
            for b in range(BATCH):
                in_copy(t, t, b).start()
        for b in range(BATCH):
            in_copy(0, 0, b).wait()
        mixer_norm(0)
        work_ref[0:HIST_A, :] = jnp.zeros((HIST_A, D_CONV), jnp.float32)
        for k in range(CONV_WIDTH):
            work_ref[CONV_W0 + k * BATCH:CONV_W0 + (k + 1) * BATCH, :] = cw_ref[k]
        bext_ref[0:HIST_B, :] = jnp.zeros((HIST_B, D_POOL), jnp.float32)
        for c in range(N_FF_CHUNKS):
            work_ref[gate_rows(c, 0, HIST_F), 0:FF_CHUNK] = jnp.zeros((HIST_F, FF_CHUNK), jnp.float32)
        xmid_ref[1] = jnp.zeros((ROWS, D_MODEL), jnp.float32)

    @pl.when(g + 2 < n_tiles)
    def _():
        for b in range(BATCH):
            in_copy(g + 2, lax.rem(g + 2, N_XIN), b).start()

    @pl.when(g + 1 < n_tiles)
    def _():
        for b in range(BATCH):
            in_copy(g + 1, lax.rem(g + 1, N_XIN), b).wait()

    @pl.when(g >= 3)
    def _():
        for b in range(BATCH):
            out_copy(g - 3, prev, b).wait()

    def ffn_up(c):
        cols = slice(c * FF_CHUNK, (c + 1) * FF_CHUNK)
        work_ref[gate_rows(c, HIST_F, ROWS), 0:FF_CHUNK] = jnp.dot(
            h2_ref[...], wup_ref[:, cols], preferred_element_type=jnp.float32)
        work_ref[gate_rows(c, HIST_F, ROWS), FF_CHUNK:2 * FF_CHUNK] = jnp.dot(
            h2_ref[...], wup_ref[:, D_FF + c * FF_CHUNK:D_FF + (c + 1) * FF_CHUNK],
            preferred_element_type=jnp.float32)

    def ffn_act(c):
        cols = slice(c * FF_CHUNK, (c + 1) * FF_CHUNK)
        gate = lambda k: work_ref[gate_rows(c, k * BATCH, ROWS), 0:FF_CHUNK]
        conv = fb_ref[:, cols] + fw_ref[FFN_CONV_WIDTH - 1:FFN_CONV_WIDTH, cols] * gate(FFN_CONV_WIDTH - 1)
        for k in range(FFN_CONV_WIDTH - 1):
            conv = conv + fw_ref[k:k + 1, cols] * gate(k)
        work_ref[gate_rows(c, 0, HIST_F), 0:FF_CHUNK] = work_ref[gate_rows(c, ROWS, HIST_F), 0:FF_CHUNK]
        grp, sub = divmod(c, DOWN_GROUP)
        act_ref[grp, :, sub * FF_CHUNK:(sub + 1) * FF_CHUNK] = (
            conv * jax.nn.sigmoid(conv)
            * work_ref[gate_rows(c, HIST_F, ROWS), FF_CHUNK:2 * FF_CHUNK]).astype(jnp.bfloat16)

    def ffn_down(c_lo, c_hi):
        ks = slice(c_lo * FF_CHUNK, c_hi * FF_CHUNK)
        part = jnp.dot(act_ref[c_lo // DOWN_GROUP, :, 0:(c_hi - c_lo) * FF_CHUNK], wdown_ref[ks, :],
                       preferred_element_type=jnp.float32)
        if c_hi == N_FF_CHUNKS:
            total = part if c_lo == 0 else ffn_ref[...] + part
            out = _rms_scale(xmid_ref[prev] + total) * gfin_ref[...]
            obuf_ref[prev] = out.reshape(TT, BATCH, D_MODEL)
        elif c_lo == 0:
            ffn_ref[...] = part
        else:
            ffn_ref[...] += part

    def mixer_head():
        proj = jnp.dot(h_ref[...], win_ref[:, 0:2 * D_CONV], preferred_element_type=jnp.float32)
        work_ref[HIST_A:HIST_A + ROWS, :] = proj[:, 0:D_CONV] * jax.nn.sigmoid(proj[:, D_CONV:2 * D_CONV])

    row_zero = pl.multiple_of(zero_ref[0], BATCH)

    def conv_rows(start):
        return pl.ds(row_zero + start, BATCH)

    def conv_block(ci, lc):
        r0 = ci * CONV_ROWS
        lanes = slice(lc * LANES, (lc + 1) * LANES)
        acc = [jnp.broadcast_to(cb_ref[:, lanes], (BATCH, LANES))] * CONV_STEPS
        for k0 in range(0, CONV_WIDTH, CONV_TAPS):
            taps = range(k0, min(k0 + CONV_TAPS, CONV_WIDTH))
            w = {k: work_ref[conv_rows(CONV_W0 + k * BATCH), lanes] for k in taps}
            u = {j: work_ref[conv_rows(r0 + j * BATCH), lanes]
                 for j in range(taps[0], taps[-1] + CONV_STEPS)}
            for k in taps:
                for t in range(CONV_STEPS):
                    acc[t] = acc[t] + w[k] * u[t + k]
        for t in range(CONV_STEPS):
            out0 = CONV_OUT0 + r0 + t * BATCH
            work_ref[out0:out0 + BATCH, lanes] = acc[t]

    def conv_norm(ci):
        r0 = ci * CONV_ROWS
        acc = work_ref[CONV_OUT0 + r0:CONV_OUT0 + r0 + CONV_ROWS, :]
        mu = jnp.mean(acc, axis=-1, keepdims=True)
        xc = acc - mu
        var = jnp.mean(xc * xc, axis=-1, keepdims=True)
        y = xc * lax.rsqrt(var + EPS) * lng_ref[...] + lnb_ref[...]
        mix_ref[r0:r0 + CONV_ROWS, 0:D_CONV] = (y * jax.nn.sigmoid(y)).astype(jnp.bfloat16)

    def pool_in():
        bext_ref[HIST_B:HIST_B + ROWS, :] = jnp.dot(
            h_ref[...], win_ref[:, 2 * D_CONV:D_IN], preferred_element_type=jnp.float32)

    def pool_group(gi):
        w = POOL_WINDOWS[gi]
        c0 = gi * POOL_GROUP_DIM
        step = lax.broadcasted_iota(jnp.int32, (ROWS, 1), 0) // BATCH
        pos = (g * TT + 1 + step).astype(jnp.float32)
        tok = bext_ref[HIST_B:HIST_B + ROWS, c0:c0 + POOL_GROUP_DIM]
        wsum = tok
        for i in range(1, w):
            lo = HIST_B - i * BATCH
            wsum = wsum + bext_ref[lo:lo + ROWS, c0:c0 + POOL_GROUP_DIM]
        d = wsum / jnp.minimum(pos, float(w)) - tok
        yb = jnp.dot(d.astype(jnp.bfloat16), pw_ref[gi], preferred_element_type=jnp.float32)
        yb = yb * ps_ref[:, c0:c0 + POOL_GROUP_DIM]
        mix_ref[:, D_CONV + c0:D_CONV + c0 + POOL_GROUP_DIM] = yb.astype(jnp.bfloat16)

    def mixer_tail():
        work_ref[0:HIST_A, :] = work_ref[ROWS:ROWS + HIST_A, :]
        bext_ref[0:HIST_B, :] = bext_ref[ROWS:ROWS + HIST_B, :]
        xmid_ref[slot] = x_tile(xslot) + jnp.dot(mix_ref[...], wout_ref[...],
                                                 preferred_element_type=jnp.float32)

    conv_pieces = [functools.partial(piece, ci)
                   for ci in range(N_CONV_CHUNKS)
                   for piece in ([functools.partial(conv_block, lc=lc) for lc in range(D_CONV // LANES)]
                                 + [conv_norm])]
    mixer_pieces = (conv_pieces + [pool_in]
                    + [functools.partial(pool_group, gi) for gi in range(len(POOL_WINDOWS))])
    mixer_slots = _deal(mixer_pieces, N_FF_CHUNKS - 1) + [[]]

    mixer_head()
    h2_ref[...] = (_rms_scale(xmid_ref[prev]) * gffn_ref[...]).astype(jnp.bfloat16)
    for c in range(min(UP_LOOKAHEAD, N_FF_CHUNKS)):
        ffn_up(c)
    for c in range(N_FF_CHUNKS):
        for piece in mixer_slots[c]:
            piece()
        if c + UP_LOOKAHEAD < N_FF_CHUNKS:
            ffn_up(c + UP_LOOKAHEAD)
        ffn_act(c)
        if (c + 1) % DOWN_GROUP == 0 or c + 1 == N_FF_CHUNKS:
            ffn_down(c // DOWN_GROUP * DOWN_GROUP, c + 1)
    mixer_norm(lax.rem(g + 1, N_XIN))
    mixer_tail()

    @pl.when(g >= 1)
    def _():
        for b in range(BATCH):
            out_copy(g - 1, prev, b).start()

    @pl.when(g == n_tiles)
    def _():
        for b in range(BATCH):
            out_copy(g - 2, slot, b).wait()
        for b in range(BATCH):
            out_copy(g - 1, prev, b).wait()


def _resident(shape):
    zeros = (0,) * len(shape)
    return pl.BlockSpec(shape, lambda g: zeros, pipeline_mode=pl.Buffered(1))


@jax.jit
def kernel(x, norm_mix_g, w_in, conv_a_w, conv_a_b, ln_a_g, ln_a_b, pool_w, pool_scale, w_out, norm_ffn_g, w_up, conv_f_w, conv_f_b, w_down, norm_final_g):
    batch, seq, d_model = x.shape
    assert batch == BATCH and d_model == D_MODEL and seq % TT == 0
    assert w_in.shape[0] == 1, "one layer"
    bf16 = jnp.bfloat16
    f32 = jnp.float32
    n_tiles = seq // TT
    assert n_tiles >= N_XIN
    row = lambda v: v.reshape(1, -1)

    operands = (
        jnp.zeros((1,), jnp.int32), x, row(norm_mix_g[0]), w_in[0].astype(bf16),
        jnp.broadcast_to(conv_a_w[0][:, None, :], (CONV_WIDTH, BATCH, D_CONV)),
        row(conv_a_b[0]), row(ln_a_g[0]), row(ln_a_b[0]), pool_w[0].astype(bf16), row(pool_scale[0]),
        w_out[0].astype(bf16), row(norm_ffn_g[0]),
        w_up[0].astype(bf16),
        conv_f_w[0], row(conv_f_b[0]), w_down[0].astype(bf16), row(norm_final_g),
    )
    in_specs = ([pl.BlockSpec(memory_space=pltpu.SMEM), pl.BlockSpec(memory_space=pl.ANY)]
                + [_resident(op.shape) for op in operands[2:]])

    return pl.pallas_call(
        functools.partial(_block_kernel, n_tiles),
        out_shape=jax.ShapeDtypeStruct(x.shape, x.dtype),
        grid=(n_tiles + 1,),
        in_specs=in_specs,
        out_specs=pl.BlockSpec(memory_space=pl.ANY),
        scratch_shapes=[
            pltpu.VMEM((N_XIN, TT, BATCH, D_MODEL), f32),
            pltpu.VMEM((2, TT, BATCH, D_MODEL), f32),
            pltpu.VMEM((ROWS, D_MODEL), bf16),
            pltpu.VMEM((ROWS, D_MODEL), bf16),
            pltpu.VMEM((WORK_ROWS, D_CONV), f32),
            pltpu.VMEM((HIST_B + ROWS, D_POOL), f32),
            pltpu.VMEM((N_DOWN_GROUPS, ROWS, DOWN_GROUP * FF_CHUNK), bf16),
            pltpu.VMEM((ROWS, D_MODEL), bf16),
            pltpu.VMEM((2, ROWS, D_MODEL), f32),
            pltpu.VMEM((ROWS, D_MODEL), f32),
            pltpu.SemaphoreType.DMA((N_XIN, BATCH)),
            pltpu.SemaphoreType.DMA((2, BATCH)),
        ],
        compiler_params=pltpu.CompilerParams(
            dimension_semantics=("arbitrary",),
            vmem_limit_bytes=VMEM_LIMIT_BYTES,
        ),
        name="hybrid_block",
    )(*operands)
```

```python
import functools

import jax
import jax.numpy as jnp
from jax import lax
from jax.experimental import pallas as pl
from jax.experimental.pallas import tpu as pltpu

D_MODEL = 1024
D_CONV = 512
D_POOL = 512
CONV_WIDTH = 31
POOL_WINDOWS = (2, 4, 8, 16)
POOL_GROUP_DIM = D_POOL // len(POOL_WINDOWS)
D_IN = 2 * D_CONV + D_POOL
D_FF = 2816
FFN_CONV_WIDTH = 3
EPS = 1e-6

BATCH = 8
LANES = 128
TT = 32
ROWS = TT * BATCH
N_XIN = 3
FF_CHUNK = 256
N_FF_CHUNKS = D_FF // FF_CHUNK
UP_LOOKAHEAD = 2
DOWN_GROUP = 11
N_DOWN_GROUPS = -(-N_FF_CHUNKS // DOWN_GROUP)
CONV_STEPS = 8
CONV_TAPS = 8
CONV_ROWS = CONV_STEPS * BATCH
N_CONV_CHUNKS = ROWS // CONV_ROWS
HIST_A = (CONV_WIDTH - 1) * BATCH
HIST_B = (max(POOL_WINDOWS) - 1) * BATCH
HIST_F = (FFN_CONV_WIDTH - 1) * BATCH
CONV_OUT0 = HIST_A + ROWS
CONV_W0 = CONV_OUT0 + ROWS
FFN0 = CONV_W0 + CONV_WIDTH * BATCH
FFN_BLOCK_ROWS = HIST_F + ROWS
WORK_ROWS = FFN0 + N_FF_CHUNKS * FFN_BLOCK_ROWS
VMEM_LIMIT_BYTES = 56 * 1024 * 1024


def _rms_scale(v):
    return v * lax.rsqrt(jnp.mean(v * v, axis=-1, keepdims=True) + EPS)


def _deal(pieces, n_slots):
    base, extra = divmod(len(pieces), n_slots)
    out, i = [], 0
    for s in range(n_slots):
        n = base + (1 if s < extra else 0)
        out.append(pieces[i:i + n])
        i += n
    return out


def _step(g, n_tiles,
                  zero_ref, x_hbm, gmix_ref, win_ref, cw_ref, cb_ref, lng_ref, lnb_ref,
                  pw_ref, ps_ref, wout_ref, gffn_ref, wup_ref, fw_ref, fb_ref,
                  wdown_ref, gfin_ref, o_hbm,
                  xin_ref, obuf_ref, h_ref, h2_ref, work_ref, bext_ref,
                  act_ref, mix_ref, xmid_ref, ffn_ref, sem_in, sem_out):
    slot = lax.rem(g, 2)
    prev = 1 - slot
    xslot = lax.rem(g, N_XIN)

    def in_copy(tile, s, b):
        return pltpu.make_async_copy(x_hbm.at[b, pl.ds(tile * TT, TT), :],
                                     xin_ref.at[s, :, b, :], sem_in.at[s, b])

    def out_copy(tile, s, b):
        return pltpu.make_async_copy(obuf_ref.at[s, :, b, :],
                                     o_hbm.at[b, pl.ds(tile * TT, TT), :], sem_out.at[s, b])

    def out_tile(step):
        return lax.rem(step + n_tiles - 1, n_tiles)

    def x_tile(s):
        return xin_ref[s].reshape(ROWS, D_MODEL)

    def mixer_norm(s):
        h_ref[...] = (_rms_scale(x_tile(s)) * gmix_ref[...]).astype(jnp.bfloat16)

    def gate_rows(c, start, size):
        lo = FFN0 + c * FFN_BLOCK_ROWS + start
        return slice(lo, lo + size)

    @pl.when(g == 0)
    def _():
        for t in range(2):
            for b in range(BATCH):
                in_copy(t, t, b).start()
        for b in range(BATCH):
            in_copy(0, 0, b).wait()
        mixer_norm(0)
        work_ref[0:HIST_A, :] = jnp.zeros((HIST_A, D_CONV), jnp.float32)
        for k in range(CONV_WIDTH):
            work_ref[CONV_W0 + k * BATCH:CONV_W0 + (k + 1) * BATCH, :] = cw_ref[k]
        bext_ref[0:HIST_B, :] = jnp.zeros((HIST_B, D_POOL), jnp.float32)
        for c in range(N_FF_CHUNKS):
            work_ref[gate_rows(c, 0, HIST_F), 0:FF_CHUNK] = jnp.zeros((HIST_F, FF_CHUNK), jnp.float32)
        xmid_ref[1] = jnp.zeros((ROWS, D_MODEL), jnp.float32)

    @pl.when(g >= 2)
    def _():
        for b in range(BATCH):
            out_copy(out_tile(g - 2), prev, b).wait()

    for b in range(BATCH):
        in_copy(jnp.minimum(g + 1, n_tiles - 1), lax.rem(g + 1, N_XIN), b).wait()
    for b in range(BATCH):
        in_copy(jnp.minimum(g + 2, n_tiles - 1), lax.rem(g + 2, N_XIN), b).start()

    def ffn_up(c):
        cols = slice(c * FF_CHUNK, (c + 1) * FF_CHUNK)
        work_ref[gate_rows(c, HIST_F, ROWS), 0:FF_CHUNK] = jnp.dot(
            h2_ref[...], wup_ref[:, cols], preferred_element_type=jnp.float32)
        work_ref[gate_rows(c, HIST_F, ROWS), FF_CHUNK:2 * FF_CHUNK] = jnp.dot(
            h2_ref[...], wup_ref[:, D_FF + c * FF_CHUNK:D_FF + (c + 1) * FF_CHUNK],
            preferred_element_type=jnp.float32)

    def ffn_act(c):
        cols = slice(c * FF_CHUNK, (c + 1) * FF_CHUNK)
        gate = lambda k: work_ref[gate_rows(c, k * BATCH, ROWS), 0:FF_CHUNK]
        conv = fb_ref[:, cols] + fw_ref[FFN_CONV_WIDTH - 1:FFN_CONV_WIDTH, cols] * gate(FFN_CONV_WIDTH - 1)
        for k in range(FFN_CONV_WIDTH - 1):
            conv = conv + fw_ref[k:k + 1, cols] * gate(k)
        work_ref[gate_rows(c, 0, HIST_F), 0:FF_CHUNK] = work_ref[gate_rows(c, ROWS, HIST_F), 0:FF_CHUNK]
        grp, sub = divmod(c, DOWN_GROUP)
        act_ref[grp, :, sub * FF_CHUNK:(sub + 1) * FF_CHUNK] = (
            conv * jax.nn.sigmoid(conv)
            * work_ref[gate_rows(c, HIST_F, ROWS), FF_CHUNK:2 * FF_CHUNK]).astype(jnp.bfloat16)

    def ffn_down(c_lo, c_hi):
        ks = slice(c_lo * FF_CHUNK, c_hi * FF_CHUNK)
        part = jnp.dot(act_ref[c_lo // DOWN_GROUP, :, 0:(c_hi - c_lo) * FF_CHUNK], wdown_ref[ks, :],
                       preferred_element_type=jnp.float32)
        if c_hi == N_FF_CHUNKS:
            total = part if c_lo == 0 else ffn_ref[...] + part
            out = _rms_scale(xmid_ref[prev] + total) * gfin_ref[...]
            obuf_ref[prev] = out.reshape(TT, BATCH, D_MODEL)
        elif c_lo == 0:
            ffn_ref[...] = part
        else:
            ffn_ref[...] += part

    def mixer_head():
        proj = jnp.dot(h_ref[...], win_ref[:, 0:2 * D_CONV], preferred_element_type=jnp.float32)
        work_ref[HIST_A:HIST_A + ROWS, :] = proj[:, 0:D_CONV] * jax.nn.sigmoid(proj[:, D_CONV:2 * D_CONV])

    row_zero = pl.multiple_of(zero_ref[0], BATCH)

    def conv_rows(start):
        return pl.ds(row_zero + start, BATCH)

    def conv_block(ci, lc):
        r0 = ci * CONV_ROWS
        lanes = slice(lc * LANES, (lc + 1) * LANES)
        acc = [jnp.broadcast_to(cb_ref[:, lanes], (BATCH, LANES))] * CONV_STEPS
        for k0 in range(0, CONV_WIDTH, CONV_TAPS):
            taps = range(k0, min(k0 + CONV_TAPS, CONV_WIDTH))
            w = {k: work_ref[conv_rows(CONV_W0 + k * BATCH), lanes] for k in taps}
            u = {j: work_ref[conv_rows(r0 + j * BATCH), lanes]
                 for j in range(taps[0], taps[-1] + CONV_STEPS)}
            for k in taps:
                for t in range(CONV_STEPS):
                    acc[t] = acc[t] + w[k] * u[t + k]
        for t in range(CONV_STEPS):
            out0 = CONV_OUT0 + r0 + t * BATCH
            work_ref[out0:out0 + BATCH, lanes] = acc[t]

    def conv_norm(ci):
        r0 = ci * CONV_ROWS
        acc = work_ref[CONV_OUT0 + r0:CONV_OUT0 + r0 + CONV_ROWS, :]
        mu = jnp.mean(acc, axis=-1, keepdims=True)
        xc = acc - mu
        var = jnp.mean(xc * xc, axis=-1, keepdims=True)
        y = xc * lax.rsqrt(var + EPS) * lng_ref[...] + lnb_ref[...]
        mix_ref[r0:r0 + CONV_ROWS, 0:D_CONV] = (y * jax.nn.sigmoid(y)).astype(jnp.bfloat16)

    def pool_in():
        bext_ref[HIST_B:HIST_B + ROWS, :] = jnp.dot(
            h_ref[...], win_ref[:, 2 * D_CONV:D_IN], preferred_element_type=jnp.float32)

    def pool_group(gi):
        w = POOL_WINDOWS[gi]
        c0 = gi * POOL_GROUP_DIM
        step = lax.broadcasted_iota(jnp.int32, (ROWS, 1), 0) // BATCH
        pos = (g * TT + 1 + step).astype(jnp.float32)
        tok = bext_ref[HIST_B:HIST_B + ROWS, c0:c0 + POOL_GROUP_DIM]
        wsum = tok
        for i in range(1, w):
            lo = HIST_B - i * BATCH
            wsum = wsum + bext_ref[lo:lo + ROWS, c0:c0 + POOL_GROUP_DIM]
        d = wsum / jnp.minimum(pos, float(w)) - tok
        yb = jnp.dot(d.astype(jnp.bfloat16), pw_ref[gi], preferred_element_type=jnp.float32)
        yb = yb * ps_ref[:, c0:c0 + POOL_GROUP_DIM]
        mix_ref[:, D_CONV + c0:D_CONV + c0 + POOL_GROUP_DIM] = yb.astype(jnp.bfloat16)

    def mixer_tail():
        work_ref[0:HIST_A, :] = work_ref[ROWS:ROWS + HIST_A, :]
        bext_ref[0:HIST_B, :] = bext_ref[ROWS:ROWS + HIST_B, :]
        xmid_ref[slot] = x_tile(xslot) + jnp.dot(mix_ref[...], wout_ref[...],
                                                 preferred_element_type=jnp.float32)

    conv_pieces = [functools.partial(piece, ci)
                   for ci in range(N_CONV_CHUNKS)
                   for piece in ([functools.partial(conv_block, lc=lc) for lc in range(D_CONV // LANES)]
                                 + [conv_norm])]
    mixer_pieces = (conv_pieces + [pool_in]
                    + [functools.partial(pool_group, gi) for gi in range(len(POOL_WINDOWS))])
    mixer_slots = _deal(mixer_pieces, N_FF_CHUNKS - 1) + [[]]

    mixer_head()
    h2_ref[...] = (_rms_scale(xmid_ref[prev]) * gffn_ref[...]).astype(jnp.bfloat16)
    for c in range(min(UP_LOOKAHEAD, N_FF_CHUNKS)):
        ffn_up(c)
    for c in range(N_FF_CHUNKS):
        for piece in mixer_slots[c]:
            piece()
        if c + UP_LOOKAHEAD < N_FF_CHUNKS:
            ffn_up(c + UP_LOOKAHEAD)
        ffn_act(c)
        if (c + 1) % DOWN_GROUP == 0 or c + 1 == N_FF_CHUNKS:
            ffn_down(c // DOWN_GROUP * DOWN_GROUP, c + 1)
    mixer_norm(lax.rem(g + 1, N_XIN))
    mixer_tail()
    for b in range(BATCH):
        out_copy(out_tile(g), prev, b).start()

    @pl.when(g == n_tiles)
    def _():
        for b in range(BATCH):
            out_copy(out_tile(g - 1), slot, b).wait()
        for b in range(BATCH):
            out_copy(out_tile(g), prev, b).wait()
        for b in range(BATCH):
            in_copy(n_tiles - 1, lax.rem(g + 2, N_XIN), b).wait()


def _block_kernel(n_tiles, *refs):
    def body(g, carry):
        _step(g, n_tiles, *refs)
        return carry

    lax.fori_loop(0, n_tiles + 1, body, 0)


@jax.jit
def kernel(x, norm_mix_g, w_in, conv_a_w, conv_a_b, ln_a_g, ln_a_b, pool_w, pool_scale, w_out, norm_ffn_g, w_up, conv_f_w, conv_f_b, w_down, norm_final_g):
    batch, seq, d_model = x.shape
    assert batch == BATCH and d_model == D_MODEL and seq % TT == 0
    assert w_in.shape[0] == 1, "one layer"
    bf16 = jnp.bfloat16
    f32 = jnp.float32
    n_tiles = seq // TT
    assert n_tiles >= N_XIN
    row = lambda v: v.reshape(1, -1)

    operands = (
        jnp.zeros((1,), jnp.int32), x, row(norm_mix_g[0]), w_in[0].astype(bf16),
        jnp.broadcast_to(conv_a_w[0][:, None, :], (CONV_WIDTH, BATCH, D_CONV)),
        row(conv_a_b[0]), row(ln_a_g[0]), row(ln_a_b[0]), pool_w[0].astype(bf16), row(pool_scale[0]),
        w_out[0].astype(bf16), row(norm_ffn_g[0]),
        w_up[0].astype(bf16),
        conv_f_w[0], row(conv_f_b[0]), w_down[0].astype(bf16), row(norm_final_g),
    )
    in_specs = ([pl.BlockSpec(memory_space=pltpu.SMEM), pl.BlockSpec(memory_space=pl.ANY)]
                + [pl.BlockSpec(memory_space=pltpu.VMEM)] * len(operands[2:]))

    return pl.pallas_call(
        functools.partial(_block_kernel, n_tiles),
        out_shape=jax.ShapeDtypeStruct(x.shape, x.dtype),
        in_specs=in_specs,
        out_specs=pl.BlockSpec(memory_space=pl.ANY),
        scratch_shapes=[
            pltpu.VMEM((N_XIN, TT, BATCH, D_MODEL), f32),
            pltpu.VMEM((2, TT, BATCH, D_MODEL), f32),
            pltpu.VMEM((ROWS, D_MODEL), bf16),
            pltpu.VMEM((ROWS, D_MODEL), bf16),
            pltpu.VMEM((WORK_ROWS, D_CONV), f32),
            pltpu.VMEM((HIST_B + ROWS, D_POOL), f32),
            pltpu.VMEM((N_DOWN_GROUPS, ROWS, DOWN_GROUP * FF_CHUNK), bf16),
            pltpu.VMEM((ROWS, D_MODEL), bf16),
            pltpu.VMEM((2, ROWS, D_MODEL), f32),
            pltpu.VMEM((ROWS, D_MODEL), f32),
            pltpu.SemaphoreType.DMA((N_XIN, BATCH)),
            pltpu.SemaphoreType.DMA((2, BATCH)),
        ],
        compiler_params=pltpu.CompilerParams(
            vmem_limit_bytes=VMEM_LIMIT_BYTES,
        ),
        name="hybrid_block",
    )(*operands)
```

```python
import functools

import jax
import jax.numpy as jnp
from jax import lax
from jax.experimental import pallas as pl
from jax.experimental.pallas import tpu as pltpu

D_MODEL = 1024
D_CONV = 512
D_POOL = 512
CONV_WIDTH = 31
POOL_WINDOWS = (2, 4, 8, 16)
POOL_GROUP_DIM = D_POOL // len(POOL_WINDOWS)
D_IN = 2 * D_CONV + D_POOL
D_FF = 2816
FFN_CONV_WIDTH = 3
EPS = 1e-6

BATCH = 8
LANES = 128
TT = 64
ROWS = TT * BATCH
N_XIN = 3
FF_CHUNK = 256
N_FF_CHUNKS = D_FF // FF_CHUNK
UP_LOOKAHEAD = 2
CONV_STEPS = 8
CONV_TAPS = 8
CONV_ROWS = CONV_STEPS * BATCH
N_CONV_CHUNKS = ROWS // CONV_ROWS
HIST_A = (CONV_WIDTH - 1) * BATCH
HIST_B = (max(POOL_WINDOWS) - 1) * BATCH
HIST_F = (FFN_CONV_WIDTH - 1) * BATCH
CONV_OUT0 = HIST_A + ROWS
CONV_W0 = CONV_OUT0 + ROWS
FFN0 = CONV_W0 + CONV_WIDTH * BATCH
FFN_BLOCK_ROWS = HIST_F + ROWS
WORK_ROWS = FFN0 + N_FF_CHUNKS * FFN_BLOCK_ROWS
VMEM_LIMIT_BYTES = 62 * 1024 * 1024


def _rms_scale(v):
    return v * lax.rsqrt(jnp.mean(v * v, axis=-1, keepdims=True) + EPS)


def _deal(pieces, n_slots):
    base, extra = divmod(len(pieces), n_slots)
    out, i = [], 0
    for s in range(n_slots):
        n = base + (1 if s < extra else 0)
        out.append(pieces[i:i + n])
        i += n
    return out


def _step(g, n_tiles,
          zero_ref, x_hbm, gmix_ref, win_ref, cw_ref, cb_ref, lng_ref, lnb_ref,
          pw_ref, ps_ref, wout_ref, gffn_ref, wup_ref, fw_ref, fb_ref,
          wdown_ref, gfin_ref, o_hbm,
          xin_ref, obuf_ref, h_ref, h2_ref, work_ref, bext_ref,
          act_ref, mix_ref, xmid_ref, sem_in, sem_out):
    slot = lax.rem(g, 2)
    prev = 1 - slot
    xslot = lax.rem(g, N_XIN)

    def in_copy(tile, s, b):
        return pltpu.make_async_copy(x_hbm.at[b, pl.ds(tile * TT, TT), :],
                                     xin_ref.at[s, :, b, :], sem_in.at[s, b])

    def out_copy(tile, s, b):
        return pltpu.make_async_copy(obuf_ref.at[s, :, b, :],
                                     o_hbm.at[b, pl.ds(tile * TT, TT), :], sem_out.at[s, b])

    def out_tile(step):
        return lax.rem(step + n_tiles - 1, n_tiles)

    def x_tile(s):
        return xin_ref[s].reshape(ROWS, D_MODEL)

    def mixer_norm(s):
        h_ref[...] = (_rms_scale(x_tile(s)) * gmix_ref[...]).astype(jnp.bfloat16)

    def gate_rows(c, start, size):
        lo = FFN0 + c * FFN_BLOCK_ROWS + start
        return slice(lo, lo + size)

    @pl.when(g == 0)
    def _():
        for t in range(2):
            for b in range(BATCH):
                in_copy(t, t, b).start()
        for b in range(BATCH):
            in_copy(0, 0, b).wait()
        mixer_norm(0)
        work_ref[0:HIST_A, :] = jnp.zeros((HIST_A, D_CONV), jnp.float32)
        for k in range(CONV_WIDTH):
            work_ref[CONV_W0 + k * BATCH:CONV_W0 + (k + 1) * BATCH, :] = cw_ref[k]
        bext_ref[0:HIST_B, :] = jnp.zeros((HIST_B, D_POOL), jnp.float32)
        for c in range(N_FF_CHUNKS):
            work_ref[gate_rows(c, 0, HIST_F), 0:FF_CHUNK] = jnp.zeros((HIST_F, FF_CHUNK), jnp.float32)
        xmid_ref[1] = jnp.zeros((ROWS, D_MODEL), jnp.float32)

    @pl.when(g >= 2)
    def _():
        for b in range(BATCH):
            out_copy(out_tile(g - 2), prev, b).wait()

    for b in range(BATCH):
        in_copy(jnp.minimum(g + 1, n_tiles - 1), lax.rem(g + 1, N_XIN), b).wait()
    for b in range(BATCH):
        in_copy(jnp.minimum(g + 2, n_tiles - 1), lax.rem(g + 2, N_XIN), b).start()

    def ffn_up(c):
        for half in range(2):
            work_ref[gate_rows(c, HIST_F, ROWS), half * FF_CHUNK:(half + 1) * FF_CHUNK] = jnp.dot(
                h2_ref[...], wup_ref[:, half * D_FF + c * FF_CHUNK:half * D_FF + (c + 1) * FF_CHUNK],
                preferred_element_type=jnp.float32)

    def ffn_act(c):
        cols = slice(c * FF_CHUNK, (c + 1) * FF_CHUNK)
        gate = lambda k: work_ref[gate_rows(c, k * BATCH, ROWS), 0:FF_CHUNK]
        conv = fb_ref[:, cols] + fw_ref[FFN_CONV_WIDTH - 1:FFN_CONV_WIDTH, cols] * gate(FFN_CONV_WIDTH - 1)
        for k in range(FFN_CONV_WIDTH - 1):
            conv = conv + fw_ref[k:k + 1, cols] * gate(k)
        work_ref[gate_rows(c, 0, HIST_F), 0:FF_CHUNK] = work_ref[gate_rows(c, ROWS, HIST_F), 0:FF_CHUNK]
        act_ref[:, cols] = (
            conv * jax.nn.sigmoid(conv)
            * work_ref[gate_rows(c, HIST_F, ROWS), FF_CHUNK:2 * FF_CHUNK]).astype(jnp.bfloat16)

    def ffn_down():
        ffn = jnp.dot(act_ref[...], wdown_ref[...], preferred_element_type=jnp.float32)
        out = _rms_scale(xmid_ref[prev] + ffn) * gfin_ref[...]
        obuf_ref[prev] = out.reshape(TT, BATCH, D_MODEL)

    def mixer_head():
        proj = jnp.dot(h_ref[...], win_ref[:, 0:2 * D_CONV], preferred_element_type=jnp.float32)
        work_ref[HIST_A:HIST_A + ROWS, :] = proj[:, 0:D_CONV] * jax.nn.sigmoid(proj[:, D_CONV:2 * D_CONV])

    row_zero = pl.multiple_of(zero_ref[0], BATCH)

    def conv_rows(start):
        return pl.ds(row_zero + start, BATCH)

    def conv_block(ci, lc):
        r0 = ci * CONV_ROWS
        lanes = slice(lc * LANES, (lc + 1) * LANES)
        acc = [jnp.broadcast_to(cb_ref[:, lanes], (BATCH, LANES))] * CONV_STEPS
        for k0 in range(0, CONV_WIDTH, CONV_TAPS):
            taps = range(k0, min(k0 + CONV_TAPS, CONV_WIDTH))
            w = {k: work_ref[conv_rows(CONV_W0 + k * BATCH), lanes] for k in taps}
            u = {j: work_ref[conv_rows(r0 + j * BATCH), lanes]
                 for j in range(taps[0], taps[-1] + CONV_STEPS)}
            for k in taps:
                for t in range(CONV_STEPS):
                    acc[t] = acc[t] + w[k] * u[t + k]
        for t in range(CONV_STEPS):
            out0 = CONV_OUT0 + r0 + t * BATCH
            work_ref[out0:out0 + BATCH, lanes] = acc[t]

    def conv_norm(ci):
        r0 = ci * CONV_ROWS
        acc = work_ref[CONV_OUT0 + r0:CONV_OUT0 + r0 + CONV_ROWS, :]
        mu = jnp.mean(acc, axis=-1, keepdims=True)
        xc = acc - mu
        var = jnp.mean(xc * xc, axis=-1, keepdims=True)
        y = xc * lax.rsqrt(var + EPS) * lng_ref[...] + lnb_ref[...]
        mix_ref[r0:r0 + CONV_ROWS, 0:D_CONV] = (y * jax.nn.sigmoid(y)).astype(jnp.bfloat16)

    def pool_in():
        bext_ref[HIST_B:HIST_B + ROWS, :] = jnp.dot(
            h_ref[...], win_ref[:, 2 * D_CONV:D_IN], preferred_element_type=jnp.float32)

    def pool_group(gi):
        w = POOL_WINDOWS[gi]
        c0 = gi * POOL_GROUP_DIM
        step = lax.broadcasted_iota(jnp.int32, (ROWS, 1), 0) // BATCH
        pos = (g * TT + 1 + step).astype(jnp.float32)
        tok = bext_ref[HIST_B:HIST_B + ROWS, c0:c0 + POOL_GROUP_DIM]
        wsum = tok
        for i in range(1, w):
            lo = HIST_B - i * BATCH
            wsum = wsum + bext_ref[lo:lo + ROWS, c0:c0 + POOL_GROUP_DIM]
        d = wsum / jnp.minimum(pos, float(w)) - tok
        yb = jnp.dot(d.astype(jnp.bfloat16), pw_ref[gi], preferred_element_type=jnp.float32)
        yb = yb * ps_ref[:, c0:c0 + POOL_GROUP_DIM]
        mix_ref[:, D_CONV + c0:D_CONV + c0 + POOL_GROUP_DIM] = yb.astype(jnp.bfloat16)

    def mixer_tail():
        work_ref[0:HIST_A, :] = work_ref[ROWS:ROWS + HIST_A, :]
        bext_ref[0:HIST_B, :] = bext_ref[ROWS:ROWS + HIST_B, :]
        xmid_ref[slot] = x_tile(xslot) + jnp.dot(mix_ref[...], wout_ref[...],
                                                 preferred_element_type=jnp.float32)

    conv_pieces = [functools.partial(piece, ci)
                   for ci in range(N_CONV_CHUNKS)
                   for piece in ([functools.partial(conv_block, lc=lc) for lc in range(D_CONV // LANES)]
                                 + [conv_norm])]
    mixer_pieces = (conv_pieces + [pool_in]
                    + [functools.partial(pool_group, gi) for gi in range(len(POOL_WINDOWS))])
    mixer_slots = _deal(mixer_pieces, N_FF_CHUNKS - 1) + [[]]

    mixer_head()
    h2_ref[...] = (_rms_scale(xmid_ref[prev]) * gffn_ref[...]).astype(jnp.bfloat16)
    for c in range(min(UP_LOOKAHEAD, N_FF_CHUNKS)):
        ffn_up(c)
    for c in range(N_FF_CHUNKS):
        for piece in mixer_slots[c]:
            piece()
        if c + UP_LOOKAHEAD < N_FF_CHUNKS:
            ffn_up(c + UP_LOOKAHEAD)
        ffn_act(c)
    ffn_down()
    mixer_norm(lax.rem(g + 1, N_XIN))
    mixer_tail()
    for b in range(BATCH):
        out_copy(out_tile(g), prev, b).start()

    @pl.when(g == n_tiles)
    def _():
        for b in range(BATCH):
            out_copy(out_tile(g - 1), slot, b).wait()
        for b in range(BATCH):
            out_copy(out_tile(g), prev, b).wait()
        for b in range(BATCH):
            in_copy(n_tiles - 1, lax.rem(g + 2, N_XIN), b).wait()


def _block_kernel(n_tiles, *refs):
    def body(g, carry):
        _step(g, n_tiles, *refs)
        return carry

    lax.fori_loop(0, n_tiles + 1, body, 0)


@jax.jit
def kernel(x, norm_mix_g, w_in, conv_a_w, conv_a_b, ln_a_g, ln_a_b, pool_w, pool_scale, w_out, norm_ffn_g, w_up, conv_f_w, conv_f_b, w_down, norm_final_g):
    batch, seq, d_model = x.shape
    assert batch == BATCH and d_model == D_MODEL and seq % TT == 0
    assert w_in.shape[0] == 1, "one layer"
    bf16 = jnp.bfloat16
    f32 = jnp.float32
    n_tiles = seq // TT
    assert n_tiles >= N_XIN
    row = lambda v: v.reshape(1, -1)

    operands = (
        jnp.zeros((1,), jnp.int32), x, row(norm_mix_g[0]), w_in[0].astype(bf16),
        jnp.broadcast_to(conv_a_w[0][:, None, :], (CONV_WIDTH, BATCH, D_CONV)),
        row(conv_a_b[0]), row(ln_a_g[0]), row(ln_a_b[0]), pool_w[0].astype(bf16), row(pool_scale[0]),
        w_out[0].astype(bf16), row(norm_ffn_g[0]),
        w_up[0].astype(bf16),
        conv_f_w[0], row(conv_f_b[0]), w_down[0].astype(bf16), row(norm_final_g),
    )
    in_specs = ([pl.BlockSpec(memory_space=pltpu.SMEM), pl.BlockSpec(memory_space=pl.ANY)]
                + [pl.BlockSpec(memory_space=pltpu.VMEM)] * len(operands[2:]))

    return pl.pallas_call(
        functools.partial(_block_kernel, n_tiles),
        out_shape=jax.ShapeDtypeStruct(x.shape, x.dtype),
        in_specs=in_specs,
        out_specs=pl.BlockSpec(memory_space=pl.ANY),
        scratch_shapes=[
            pltpu.VMEM((N_XIN, TT, BATCH, D_MODEL), f32),
            pltpu.VMEM((2, TT, BATCH, D_MODEL), f32),
            pltpu.VMEM((ROWS, D_MODEL), bf16),
            pltpu.VMEM((ROWS, D_MODEL), bf16),
            pltpu.VMEM((WORK_ROWS, D_CONV), f32),
            pltpu.VMEM((HIST_B + ROWS, D_POOL), f32),
            pltpu.VMEM((ROWS, D_FF), bf16),
            pltpu.VMEM((ROWS, D_MODEL), bf16),
            pltpu.VMEM((2, ROWS, D_MODEL), f32),
            pltpu.SemaphoreType.DMA((N_XIN, BATCH)),
            pltpu.SemaphoreType.DMA((2, BATCH)),
        ],
        compiler_params=pltpu.CompilerParams(
            vmem_limit_bytes=VMEM_LIMIT_BYTES,
        ),
        name="hybrid_block",
    )(*operands)
```

```python
import functools

import jax
import jax.numpy as jnp
from jax import lax
from jax.experimental import pallas as pl
from jax.experimental.pallas import tpu as pltpu

D_MODEL = 1024
D_CONV = 512
D_POOL = 512
CONV_WIDTH = 31
POOL_WINDOWS = (2, 4, 8, 16)
POOL_GROUP_DIM = D_POOL // len(POOL_WINDOWS)
D_IN = 2 * D_CONV + D_POOL
D_FF = 2816
FFN_CONV_WIDTH = 3
EPS = 1e-6

BATCH = 8
LANES = 128
TT = 32
ROWS = TT * BATCH
N_XIN = 3
FF_CHUNK = 256
N_FF_CHUNKS = D_FF // FF_CHUNK
UP_LOOKAHEAD = 2
CONV_STEPS = 8
CONV_TAPS = 8
CONV_ROWS = CONV_STEPS * BATCH
N_CONV_CHUNKS = ROWS // CONV_ROWS
HIST_A = (CONV_WIDTH - 1) * BATCH
HIST_B = (max(POOL_WINDOWS) - 1) * BATCH
HIST_F = (FFN_CONV_WIDTH - 1) * BATCH
CONV_OUT0 = HIST_A + ROWS
CONV_W0 = CONV_OUT0 + ROWS
FFN0 = CONV_W0 + CONV_WIDTH * BATCH
FFN_BLOCK_ROWS = HIST_F + ROWS
WORK_ROWS = FFN0 + N_FF_CHUNKS * FFN_BLOCK_ROWS
WEIGHT_OPERANDS = (3, 10, 12, 15)
STAGE_ROWS = {D_IN: 256, D_MODEL: 256, 2 * D_FF: 128}
STAGE_WIDTHS = tuple(STAGE_ROWS)
VMEM_LIMIT_BYTES = 56 * 1024 * 1024


def _rms_scale(v):
    return v * lax.rsqrt(jnp.mean(v * v, axis=-1, keepdims=True) + EPS)


def _deal(pieces, n_slots):
    base, extra = divmod(len(pieces), n_slots)
    out, i = [], 0
    for s in range(n_slots):
        n = base + (1 if s < extra else 0)
        out.append(pieces[i:i + n])
        i += n
    return out


def _step(g, n_tiles,
          zero_ref, x_hbm, gmix_ref, win_ref, cw_ref, cb_ref, lng_ref, lnb_ref,
          pw_ref, ps_ref, wout_ref, gffn_ref, wup_ref, fw_ref, fb_ref,
          wdown_ref, gfin_ref, o_hbm,
          xin_ref, obuf_ref, h_ref, h2_ref, work_ref, bext_ref,
          act_ref, mix_ref, xmid_ref, sem_in, sem_out):
    slot = lax.rem(g, 2)
    prev = 1 - slot
    xslot = lax.rem(g, N_XIN)

    def in_copy(tile, s, b):
        return pltpu.make_async_copy(x_hbm.at[b, pl.ds(tile * TT, TT), :],
                                     xin_ref.at[s, :, b, :], sem_in.at[s, b])

    def out_copy(tile, s, b):
        return pltpu.make_async_copy(obuf_ref.at[s, :, b, :],
                                     o_hbm.at[b, pl.ds(tile * TT, TT), :], sem_out.at[s, b])

    def out_tile(step):
        return lax.rem(step + n_tiles - 1, n_tiles)

    def x_tile(s):
        return xin_ref[s].reshape(ROWS, D_MODEL)

    def mixer_norm(s):
        h_ref[...] = (_rms_scale(x_tile(s)) * gmix_ref[...]).astype(jnp.bfloat16)

    def gate_rows(c, start, size):
        lo = FFN0 + c * FFN_BLOCK_ROWS + start
        return slice(lo, lo + size)

    @pl.when(g == 0)
    def _():
        for t in range(2):
            for b in range(BATCH):
                in_copy(t, t, b).start()
        for b in range(BATCH):
            in_copy(0, 0, b).wait()
        mixer_norm(0)
        work_ref[0:HIST_A, :] = jnp.zeros((HIST_A, D_CONV), jnp.float32)
        for k in range(CONV_WIDTH):
            work_ref[CONV_W0 + k * BATCH:CONV_W0 + (k + 1) * BATCH, :] = cw_ref[k]
        bext_ref[0:HIST_B, :] = jnp.zeros((HIST_B, D_POOL), jnp.float32)
        for c in range(N_FF_CHUNKS):
            work_ref[gate_rows(c, 0, HIST_F), 0:FF_CHUNK] = jnp.zeros((HIST_F, FF_CHUNK), jnp.float32)
        xmid_ref[1] = jnp.zeros((ROWS, D_MODEL), jnp.float32)

    @pl.when(g >= 2)
    def _():
        for b in range(BATCH):
            out_copy(out_tile(g - 2), prev, b).wait()

    for b in range(BATCH):
        in_copy(jnp.minimum(g + 1, n_tiles - 1), lax.rem(g + 1, N_XIN), b).wait()
    for b in range(BATCH):
        in_copy(jnp.minimum(g + 2, n_tiles - 1), lax.rem(g + 2, N_XIN), b).start()

    def ffn_up(c):
        for half in range(2):
            work_ref[gate_rows(c, HIST_F, ROWS), half * FF_CHUNK:(half + 1) * FF_CHUNK] = jnp.dot(
                h2_ref[...], wup_ref[:, half * D_FF + c * FF_CHUNK:half * D_FF + (c + 1) * FF_CHUNK],
                preferred_element_type=jnp.float32)

    def ffn_act(c):
        cols = slice(c * FF_CHUNK, (c + 1) * FF_CHUNK)
        gate = lambda k: work_ref[gate_rows(c, k * BATCH, ROWS), 0:FF_CHUNK]
        conv = fb_ref[:, cols] + fw_ref[FFN_CONV_WIDTH - 1:FFN_CONV_WIDTH, cols] * gate(FFN_CONV_WIDTH - 1)
        for k in range(FFN_CONV_WIDTH - 1):
            conv = conv + fw_ref[k:k + 1, cols] * gate(k)
        work_ref[gate_rows(c, 0, HIST_F), 0:FF_CHUNK] = work_ref[gate_rows(c, ROWS, HIST_F), 0:FF_CHUNK]
        act_ref[:, cols] = (
            conv * jax.nn.sigmoid(conv)
            * work_ref[gate_rows(c, HIST_F, ROWS), FF_CHUNK:2 * FF_CHUNK]).astype(jnp.bfloat16)

    def ffn_down():
        ffn = jnp.dot(act_ref[...], wdown_ref[...], preferred_element_type=jnp.float32)
        out = _rms_scale(xmid_ref[prev] + ffn) * gfin_ref[...]
        obuf_ref[prev] = out.reshape(TT, BATCH, D_MODEL)

    def mixer_head():
        proj = jnp.dot(h_ref[...], win_ref[:, 0:2 * D_CONV], preferred_element_type=jnp.float32)
        work_ref[HIST_A:HIST_A + ROWS, :] = proj[:, 0:D_CONV] * jax.nn.sigmoid(proj[:, D_CONV:2 * D_CONV])

    row_zero = pl.multiple_of(zero_ref[0], BATCH)

    def conv_rows(start):
        return pl.ds(row_zero + start, BATCH)

    def conv_block(ci, lc):
        r0 = ci * CONV_ROWS
        lanes = slice(lc * LANES, (lc + 1) * LANES)
        acc = [jnp.broadcast_to(cb_ref[:, lanes], (BATCH, LANES))] * CONV_STEPS
        for k0 in range(0, CONV_WIDTH, CONV_TAPS):
            taps = range(k0, min(k0 + CONV_TAPS, CONV_WIDTH))
            w = {k: work_ref[conv_rows(CONV_W0 + k * BATCH), lanes] for k in taps}
            u = {j: work_ref[conv_rows(r0 + j * BATCH), lanes]
                 for j in range(taps[0], taps[-1] + CONV_STEPS)}
            for k in taps:
                for t in range(CONV_STEPS):
                    acc[t] = acc[t] + w[k] * u[t + k]
        for t in range(CONV_STEPS):
            out0 = CONV_OUT0 + r0 + t * BATCH
            work_ref[out0:out0 + BATCH, lanes] = acc[t]

    def conv_norm(ci):
        r0 = ci * CONV_ROWS
        acc = work_ref[CONV_OUT0 + r0:CONV_OUT0 + r0 + CONV_ROWS, :]
        mu = jnp.mean(acc, axis=-1, keepdims=True)
        xc = acc - mu
        var = jnp.mean(xc * xc, axis=-1, keepdims=True)
        y = xc * lax.rsqrt(var + EPS) * lng_ref[...] + lnb_ref[...]
        mix_ref[r0:r0 + CONV_ROWS, 0:D_CONV] = (y * jax.nn.sigmoid(y)).astype(jnp.bfloat16)

    def pool_in():
        bext_ref[HIST_B:HIST_B + ROWS, :] = jnp.dot(
            h_ref[...], win_ref[:, 2 * D_CONV:D_IN], preferred_element_type=jnp.float32)

    def pool_group(gi):
        w = POOL_WINDOWS[gi]
        c0 = gi * POOL_GROUP_DIM
        step = lax.broadcasted_iota(jnp.int32, (ROWS, 1), 0) // BATCH
        pos = (g * TT + 1 + step).astype(jnp.float32)
        tok = bext_ref[HIST_B:HIST_B + ROWS, c0:c0 + POOL_GROUP_DIM]
        wsum = tok
        for i in range(1, w):
            lo = HIST_B - i * BATCH
            wsum = wsum + bext_ref[lo:lo + ROWS, c0:c0 + POOL_GROUP_DIM]
        d = wsum / jnp.minimum(pos, float(w)) - tok
        yb = jnp.dot(d.astype(jnp.bfloat16), pw_ref[gi], preferred_element_type=jnp.float32)
        yb = yb * ps_ref[:, c0:c0 + POOL_GROUP_DIM]
        mix_ref[:, D_CONV + c0:D_CONV + c0 + POOL_GROUP_DIM] = yb.astype(jnp.bfloat16)

    def mixer_tail():
        work_ref[0:HIST_A, :] = work_ref[ROWS:ROWS + HIST_A, :]
        bext_ref[0:HIST_B, :] = bext_ref[ROWS:ROWS + HIST_B, :]
        xmid_ref[slot] = x_tile(xslot) + jnp.dot(mix_ref[...], wout_ref[...],
                                                 preferred_element_type=jnp.float32)

    conv_pieces = [functools.partial(piece, ci)
                   for ci in range(N_CONV_CHUNKS)
                   for piece in ([functools.partial(conv_block, lc=lc) for lc in range(D_CONV // LANES)]
                                 + [conv_norm])]
    mixer_pieces = (conv_pieces + [pool_in]
                    + [functools.partial(pool_group, gi) for gi in range(len(POOL_WINDOWS))])
    mixer_slots = _deal(mixer_pieces, N_FF_CHUNKS - 1) + [[]]

    mixer_head()
    h2_ref[...] = (_rms_scale(xmid_ref[prev]) * gffn_ref[...]).astype(jnp.bfloat16)
    for c in range(min(UP_LOOKAHEAD, N_FF_CHUNKS)):
        ffn_up(c)
    for c in range(N_FF_CHUNKS):
        for piece in mixer_slots[c]:
            piece()
        if c + UP_LOOKAHEAD < N_FF_CHUNKS:
            ffn_up(c + UP_LOOKAHEAD)
        ffn_act(c)
    ffn_down()
    mixer_norm(lax.rem(g + 1, N_XIN))
    mixer_tail()
    for b in range(BATCH):
        out_copy(out_tile(g), prev, b).start()

    @pl.when(g == n_tiles)
    def _():
        for b in range(BATCH):
            out_copy(out_tile(g - 1), slot, b).wait()
        for b in range(BATCH):
            out_copy(out_tile(g), prev, b).wait()
        for b in range(BATCH):
            in_copy(n_tiles - 1, lax.rem(g + 2, N_XIN), b).wait()


def _load_weight(w_hbm, w_ref, stage_ref, sem):
    rows = stage_ref.shape[1]
    n_chunks = w_hbm.shape[0] // rows
    assert n_chunks * rows == w_hbm.shape[0]

    def chunk_copy(i, s):
        return pltpu.make_async_copy(w_hbm.at[pl.ds(i * rows, rows), :], stage_ref.at[s], sem.at[s])

    chunk_copy(0, 0).start()

    def body(i, carry):
        s = lax.rem(i, 2)

        @pl.when(i + 1 < n_chunks)
        def _():
            chunk_copy(i + 1, 1 - s).start()

        chunk_copy(i, s).wait()
        w_ref[pl.ds(pl.multiple_of(i * rows, rows), rows), :] = stage_ref[s].astype(jnp.bfloat16)
        return carry

    lax.fori_loop(0, n_chunks, body, 0)


def _block_kernel(n_tiles, *refs):
    refs = list(refs)
    n_w = len(WEIGHT_OPERANDS)
    sem_w = refs.pop()
    stages = [refs.pop() for _ in range(len(STAGE_WIDTHS))][::-1]
    w_vmem = [refs.pop() for _ in range(n_w)][::-1]
    for pos, w_ref in zip(WEIGHT_OPERANDS, w_vmem):
        stage = stages[STAGE_WIDTHS.index(w_ref.shape[1])]
        _load_weight(refs[pos], w_ref, stage, sem_w)
        refs[pos] = w_ref

    def body(g, carry):
        _step(g, n_tiles, *refs)
        return carry

    lax.fori_loop(0, n_tiles + 1, body, 0)


@jax.jit
def kernel(x, norm_mix_g, w_in, conv_a_w, conv_a_b, ln_a_g, ln_a_b, pool_w, pool_scale, w_out, norm_ffn_g, w_up, conv_f_w, conv_f_b, w_down, norm_final_g):
    batch, seq, d_model = x.shape
    assert batch == BATCH and d_model == D_MODEL and seq % TT == 0
    assert w_in.shape[0] == 1, "one layer"
    bf16 = jnp.bfloat16
    f32 = jnp.float32
    n_tiles = seq // TT
    assert n_tiles >= N_XIN
    row = lambda v: v.reshape(1, -1)

    operands = (
        jnp.zeros((1,), jnp.int32), x, row(norm_mix_g[0]), w_in[0],
        jnp.broadcast_to(conv_a_w[0][:, None, :], (CONV_WIDTH, BATCH, D_CONV)),
        row(conv_a_b[0]), row(ln_a_g[0]), row(ln_a_b[0]), pool_w[0].astype(bf16), row(pool_scale[0]),
        w_out[0], row(norm_ffn_g[0]),
        w_up[0],
        conv_f_w[0], row(conv_f_b[0]), w_down[0], row(norm_final_g),
    )
    in_specs = ([pl.BlockSpec(memory_space=pltpu.SMEM), pl.BlockSpec(memory_space=pl.ANY)]
                + [pl.BlockSpec(memory_space=pltpu.VMEM)] * len(operands[2:]))
    for pos in WEIGHT_OPERANDS:
        in_specs[pos] = pl.BlockSpec(memory_space=pl.ANY)

    return pl.pallas_call(
        functools.partial(_block_kernel, n_tiles),
        out_shape=jax.ShapeDtypeStruct(x.shape, x.dtype),
        in_specs=in_specs,
        out_specs=pl.BlockSpec(memory_space=pl.ANY),
        scratch_shapes=[
            pltpu.VMEM((N_XIN, TT, BATCH, D_MODEL), f32),
            pltpu.VMEM((2, TT, BATCH, D_MODEL), f32),
            pltpu.VMEM((ROWS, D_MODEL), bf16),
            pltpu.VMEM((ROWS, D_MODEL), bf16),
            pltpu.VMEM((WORK_ROWS, D_CONV), f32),
            pltpu.VMEM((HIST_B + ROWS, D_POOL), f32),
            pltpu.VMEM((ROWS, D_FF), bf16),
            pltpu.VMEM((ROWS, D_MODEL), bf16),
            pltpu.VMEM((2, ROWS, D_MODEL), f32),
            pltpu.SemaphoreType.DMA((N_XIN, BATCH)),
            pltpu.SemaphoreType.DMA((2, BATCH)),
        ] + [pltpu.VMEM(operands[pos].shape, bf16) for pos in WEIGHT_OPERANDS] + [
            pltpu.VMEM((2, STAGE_ROWS[width], width), f32) for width in STAGE_WIDTHS
        ] + [pltpu.SemaphoreType.DMA((2,))],
        compiler_params=pltpu.CompilerParams(
            vmem_limit_bytes=VMEM_LIMIT_BYTES,
        ),
        name="hybrid_block",
    )(*operands)
```

```python
import functools

import jax
import jax.numpy as jnp
from jax import lax
from jax.experimental import pallas as pl
from jax.experimental.pallas import tpu as pltpu

D_MODEL = 1024
D_CONV = 512
D_POOL = 512
CONV_WIDTH = 31
POOL_WINDOWS = (2, 4, 8, 16)
POOL_GROUP_DIM = D_POOL // len(POOL_WINDOWS)
D_IN = 2 * D_CONV + D_POOL
D_FF = 2816
FFN_CONV_WIDTH = 3
EPS = 1e-6

BATCH = 8
LANES = 128
TT = 64
ROWS = TT * BATCH
N_XIN = 3
FF_CHUNK = 256
N_FF_CHUNKS = D_FF // FF_CHUNK
UP_LOOKAHEAD = 2
CONV_STEPS = 8
CONV_TAPS = 8
CONV_ROWS = CONV_STEPS * BATCH
N_CONV_CHUNKS = ROWS // CONV_ROWS
HIST_A = (CONV_WIDTH - 1) * BATCH
HIST_B = (max(POOL_WINDOWS) - 1) * BATCH
HIST_F = (FFN_CONV_WIDTH - 1) * BATCH
CONV_OUT0 = HIST_A + ROWS
CONV_W0 = CONV_OUT0 + ROWS
FFN0 = CONV_W0 + CONV_WIDTH * BATCH
FFN_BLOCK_ROWS = HIST_F + ROWS
WORK_ROWS = FFN0 + N_FF_CHUNKS * FFN_BLOCK_ROWS
WEIGHT_OPERANDS = (3, 10, 12, 15)
STAGE_ROWS = {D_IN: 256, D_MODEL: 256, 2 * D_FF: 128}
STAGE_WIDTHS = tuple(STAGE_ROWS)
VMEM_LIMIT_BYTES = 62 * 1024 * 1024


def _rms_scale(v):
    return v * lax.rsqrt(jnp.mean(v * v, axis=-1, keepdims=True) + EPS)


def _deal(pieces, n_slots):
    base, extra = divmod(len(pieces), n_slots)
    out, i = [], 0
    for s in range(n_slots):
        n = base + (1 if s < extra else 0)
        out.append(pieces[i:i + n])
        i += n
    return out


def _step(g, n_tiles,
          zero_ref, x_hbm, gmix_ref, win_ref, cw_ref, cb_ref, lng_ref, lnb_ref,
          pw_ref, ps_ref, wout_ref, gffn_ref, wup_ref, fw_ref, fb_ref,
          wdown_ref, gfin_ref, o_hbm,
          xin_ref, obuf_ref, h_ref, h2_ref, work_ref, bext_ref,
          act_ref, mix_ref, xmid_ref, sem_in, sem_out):
    slot = lax.rem(g, 2)
    prev = 1 - slot
    xslot = lax.rem(g, N_XIN)

    def in_copy(tile, s, b):
        return pltpu.make_async_copy(x_hbm.at[b, pl.ds(tile * TT, TT), :],
                                     xin_ref.at[s, :, b, :], sem_in.at[s, b])

    def out_copy(tile, s, b):
        return pltpu.make_async_copy(obuf_ref.at[s, :, b, :],
                                     o_hbm.at[b, pl.ds(tile * TT, TT), :], sem_out.at[s, b])

    def out_tile(step):
        return lax.rem(step + n_tiles - 1, n_tiles)

    def x_tile(s):
        return xin_ref[s].reshape(ROWS, D_MODEL)

    def mixer_norm(s):
        h_ref[...] = (_rms_scale(x_tile(s)) * gmix_ref[...]).astype(jnp.bfloat16)

    def gate_rows(c, start, size):
        lo = FFN0 + c * FFN_BLOCK_ROWS + start
        return slice(lo, lo + size)

    @pl.when(g == 0)
    def _():
        for t in range(2):
            for b in range(BATCH):
                in_copy(t, t, b).start()
        for b in range(BATCH):
            in_copy(0, 0, b).wait()
        mixer_norm(0)
        work_ref[0:HIST_A, :] = jnp.zeros((HIST_A, D_CONV), jnp.float32)
        for k in range(CONV_WIDTH):
            work_ref[CONV_W0 + k * BATCH:CONV_W0 + (k + 1) * BATCH, :] = cw_ref[k]
        bext_ref[0:HIST_B, :] = jnp.zeros((HIST_B, D_POOL), jnp.float32)
        for c in range(N_FF_CHUNKS):
            work_ref[gate_rows(c, 0, HIST_F), 0:FF_CHUNK] = jnp.zeros((HIST_F, FF_CHUNK), jnp.float32)
        xmid_ref[1] = jnp.zeros((ROWS, D_MODEL), jnp.float32)

    @pl.when(g >= 2)
    def _():
        for b in range(BATCH):
            out_copy(out_tile(g - 2), prev, b).wait()

    for b in range(BATCH):
        in_copy(jnp.minimum(g + 1, n_tiles - 1), lax.rem(g + 1, N_XIN), b).wait()
    for b in range(BATCH):
        in_copy(jnp.minimum(g + 2, n_tiles - 1), lax.rem(g + 2, N_XIN), b).start()

    def ffn_up(c):
        for half in range(2):
            work_ref[gate_rows(c, HIST_F, ROWS), half * FF_CHUNK:(half + 1) * FF_CHUNK] = jnp.dot(
                h2_ref[...], wup_ref[:, half * D_FF + c * FF_CHUNK:half * D_FF + (c + 1) * FF_CHUNK],
                preferred_element_type=jnp.float32)

    def ffn_act(c):
        cols = slice(c * FF_CHUNK, (c + 1) * FF_CHUNK)
        gate = lambda k: work_ref[gate_rows(c, k * BATCH, ROWS), 0:FF_CHUNK]
        conv = fb_ref[:, cols] + fw_ref[FFN_CONV_WIDTH - 1:FFN_CONV_WIDTH, cols] * gate(FFN_CONV_WIDTH - 1)
        for k in range(FFN_CONV_WIDTH - 1):
            conv = conv + fw_ref[k:k + 1, cols] * gate(k)
        work_ref[gate_rows(c, 0, HIST_F), 0:FF_CHUNK] = work_ref[gate_rows(c, ROWS, HIST_F), 0:FF_CHUNK]
        act_ref[:, cols] = (
            conv * jax.nn.sigmoid(conv)
            * work_ref[gate_rows(c, HIST_F, ROWS), FF_CHUNK:2 * FF_CHUNK]).astype(jnp.bfloat16)

    def ffn_down():
        ffn = jnp.dot(act_ref[...], wdown_ref[...], preferred_element_type=jnp.float32)
        out = _rms_scale(xmid_ref[prev] + ffn) * gfin_ref[...]
        obuf_ref[prev] = out.reshape(TT, BATCH, D_MODEL)

    def mixer_head():
        proj = jnp.dot(h_ref[...], win_ref[:, 0:2 * D_CONV], preferred_element_type=jnp.float32)
        work_ref[HIST_A:HIST_A + ROWS, :] = proj[:, 0:D_CONV] * jax.nn.sigmoid(proj[:, D_CONV:2 * D_CONV])

    row_zero = pl.multiple_of(zero_ref[0], BATCH)

    def conv_rows(start):
        return pl.ds(row_zero + start, BATCH)

    def conv_block(ci, lc):
        r0 = ci * CONV_ROWS
        lanes = slice(lc * LANES, (lc + 1) * LANES)
        acc = [jnp.broadcast_to(cb_ref[:, lanes], (BATCH, LANES))] * CONV_STEPS
        for k0 in range(0, CONV_WIDTH, CONV_TAPS):
            taps = range(k0, min(k0 + CONV_TAPS, CONV_WIDTH))
            w = {k: work_ref[conv_rows(CONV_W0 + k * BATCH), lanes] for k in taps}
            u = {j: work_ref[conv_rows(r0 + j * BATCH), lanes]
                 for j in range(taps[0], taps[-1] + CONV_STEPS)}
            for k in taps:
                for t in range(CONV_STEPS):
                    acc[t] = acc[t] + w[k] * u[t + k]
        for t in range(CONV_STEPS):
            out0 = CONV_OUT0 + r0 + t * BATCH
            work_ref[out0:out0 + BATCH, lanes] = acc[t]

    def conv_norm(ci):
        r0 = ci * CONV_ROWS
        acc = work_ref[CONV_OUT0 + r0:CONV_OUT0 + r0 + CONV_ROWS, :]
        mu = jnp.mean(acc, axis=-1, keepdims=True)
        xc = acc - mu
        var = jnp.mean(xc * xc, axis=-1, keepdims=True)
        y = xc * lax.rsqrt(var + EPS) * lng_ref[...] + lnb_ref[...]
        mix_ref[r0:r0 + CONV_ROWS, 0:D_CONV] = (y * jax.nn.sigmoid(y)).astype(jnp.bfloat16)

    def pool_in():
        bext_ref[HIST_B:HIST_B + ROWS, :] = jnp.dot(
            h_ref[...], win_ref[:, 2 * D_CONV:D_IN], preferred_element_type=jnp.float32)

    def pool_group(gi):
        w = POOL_WINDOWS[gi]
        c0 = gi * POOL_GROUP_DIM
        step = lax.broadcasted_iota(jnp.int32, (ROWS, 1), 0) // BATCH
        pos = (g * TT + 1 + step).astype(jnp.float32)
        tok = bext_ref[HIST_B:HIST_B + ROWS, c0:c0 + POOL_GROUP_DIM]
        wsum = tok
        for i in range(1, w):
            lo = HIST_B - i * BATCH
            wsum = wsum + bext_ref[lo:lo + ROWS, c0:c0 + POOL_GROUP_DIM]
        d = wsum / jnp.minimum(pos, float(w)) - tok
        yb = jnp.dot(d.astype(jnp.bfloat16), pw_ref[gi], preferred_element_type=jnp.float32)
        yb = yb * ps_ref[:, c0:c0 + POOL_GROUP_DIM]
        mix_ref[:, D_CONV + c0:D_CONV + c0 + POOL_GROUP_DIM] = yb.astype(jnp.bfloat16)

    def mixer_tail():
        work_ref[0:HIST_A, :] = work_ref[ROWS:ROWS + HIST_A, :]
        bext_ref[0:HIST_B, :] = bext_ref[ROWS:ROWS + HIST_B, :]
        xmid_ref[slot] = x_tile(xslot) + jnp.dot(mix_ref[...], wout_ref[...],
                                                 preferred_element_type=jnp.float32)

    conv_pieces = [functools.partial(piece, ci)
                   for ci in range(N_CONV_CHUNKS)
                   for piece in ([functools.partial(conv_block, lc=lc) for lc in range(D_CONV // LANES)]
                                 + [conv_norm])]
    mixer_pieces = (conv_pieces + [pool_in]
                    + [functools.partial(pool_group, gi) for gi in range(len(POOL_WINDOWS))])
    mixer_slots = _deal(mixer_pieces, N_FF_CHUNKS - 1) + [[]]

    mixer_head()
    h2_ref[...] = (_rms_scale(xmid_ref[prev]) * gffn_ref[...]).astype(jnp.bfloat16)
    for c in range(min(UP_LOOKAHEAD, N_FF_CHUNKS)):
        ffn_up(c)
    for c in range(N_FF_CHUNKS):
        for piece in mixer_slots[c]:
            piece()
        if c + UP_LOOKAHEAD < N_FF_CHUNKS:
            ffn_up(c + UP_LOOKAHEAD)
        ffn_act(c)
    ffn_down()
    mixer_norm(lax.rem(g + 1, N_XIN))
    mixer_tail()
    for b in range(BATCH):
        out_copy(out_tile(g), prev, b).start()

    @pl.when(g == n_tiles)
    def _():
        for b in range(BATCH):
            out_copy(out_tile(g - 1), slot, b).wait()
        for b in range(BATCH):
            out_copy(out_tile(g), prev, b).wait()
        for b in range(BATCH):
            in_copy(n_tiles - 1, lax.rem(g + 2, N_XIN), b).wait()


def _load_weight(w_hbm, w_ref, stage_ref, sem):
    rows = stage_ref.shape[1]
    n_chunks = w_hbm.shape[0] // rows
    assert n_chunks * rows == w_hbm.shape[0]

    def chunk_copy(i, s):
        return pltpu.make_async_copy(w_hbm.at[pl.ds(i * rows, rows), :], stage_ref.at[s], sem.at[s])

    chunk_copy(0, 0).start()

    def body(i, carry):
        s = lax.rem(i, 2)

        @pl.when(i + 1 < n_chunks)
        def _():
            chunk_copy(i + 1, 1 - s).start()

        chunk_copy(i, s).wait()
        w_ref[pl.ds(pl.multiple_of(i * rows, rows), rows), :] = stage_ref[s].astype(jnp.bfloat16)
        return carry

    lax.fori_loop(0, n_chunks, body, 0)


def _step_scratch():
    f32, bf16 = jnp.float32, jnp.bfloat16
    return [
        pltpu.VMEM((N_XIN, TT, BATCH, D_MODEL), f32),
        pltpu.VMEM((2, TT, BATCH, D_MODEL), f32),
        pltpu.VMEM((ROWS, D_MODEL), bf16),
        pltpu.VMEM((ROWS, D_MODEL), bf16),
        pltpu.VMEM((WORK_ROWS, D_CONV), f32),
        pltpu.VMEM((HIST_B + ROWS, D_POOL), f32),
        pltpu.VMEM((ROWS, D_FF), bf16),
        pltpu.VMEM((ROWS, D_MODEL), bf16),
        pltpu.VMEM((2, ROWS, D_MODEL), f32),
        pltpu.SemaphoreType.DMA((N_XIN, BATCH)),
        pltpu.SemaphoreType.DMA((2, BATCH)),
    ]


def _block_kernel(n_tiles, *refs):
    n_w = len(WEIGHT_OPERANDS)
    io, w_vmem = list(refs[:-n_w]), refs[-n_w:]

    def load_weights(*scoped):
        stages, sem_w = scoped[:-1], scoped[-1]
        for pos, w_ref in zip(WEIGHT_OPERANDS, w_vmem):
            _load_weight(io[pos], w_ref, stages[STAGE_WIDTHS.index(w_ref.shape[1])], sem_w)

    pl.run_scoped(load_weights,
                  *[pltpu.VMEM((2, STAGE_ROWS[width], width), jnp.float32) for width in STAGE_WIDTHS],
                  pltpu.SemaphoreType.DMA((2,)))
    for pos, w_ref in zip(WEIGHT_OPERANDS, w_vmem):
        io[pos] = w_ref

    def run(*step_scratch):
        def body(g, carry):
            _step(g, n_tiles, *io, *step_scratch)
            return carry

        lax.fori_loop(0, n_tiles + 1, body, 0)

    pl.run_scoped(run, *_step_scratch())


@jax.jit
def kernel(x, norm_mix_g, w_in, conv_a_w, conv_a_b, ln_a_g, ln_a_b, pool_w, pool_scale, w_out, norm_ffn_g, w_up, conv_f_w, conv_f_b, w_down, norm_final_g):
    batch, seq, d_model = x.shape
    assert batch == BATCH and d_model == D_MODEL and seq % TT == 0
    assert w_in.shape[0] == 1, "one layer"
    bf16 = jnp.bfloat16
    n_tiles = seq // TT
    assert n_tiles >= N_XIN
    row = lambda v: v.reshape(1, -1)

    operands = (
        jnp.zeros((1,), jnp.int32), x, row(norm_mix_g[0]), w_in[0],
        jnp.broadcast_to(conv_a_w[0][:, None, :], (CONV_WIDTH, BATCH, D_CONV)),
        row(conv_a_b[0]), row(ln_a_g[0]), row(ln_a_b[0]), pool_w[0].astype(bf16), row(pool_scale[0]),
        w_out[0], row(norm_ffn_g[0]),
        w_up[0],
        conv_f_w[0], row(conv_f_b[0]), w_down[0], row(norm_final_g),
    )
    in_specs = ([pl.BlockSpec(memory_space=pltpu.SMEM), pl.BlockSpec(memory_space=pl.ANY)]
                + [pl.BlockSpec(memory_space=pltpu.VMEM)] * len(operands[2:]))
    for pos in WEIGHT_OPERANDS:
        in_specs[pos] = pl.BlockSpec(memory_space=pl.ANY)

    return pl.pallas_call(
        functools.partial(_block_kernel, n_tiles),
        out_shape=jax.ShapeDtypeStruct(x.shape, x.dtype),
        in_specs=in_specs,
        out_specs=pl.BlockSpec(memory_space=pl.ANY),
        scratch_shapes=[pltpu.VMEM(operands[pos].shape, bf16) for pos in WEIGHT_OPERANDS],
        compiler_params=pltpu.CompilerParams(
            vmem_limit_bytes=VMEM_LIMIT_BYTES,
        ),
        name="hybrid_block",
    )(*operands)
```

```python
import functools

import jax
import jax.numpy as jnp
from jax import lax
from jax.experimental import pallas as pl
from jax.experimental.pallas import tpu as pltpu

D_MODEL = 1024
D_CONV = 512
D_POOL = 512
CONV_WIDTH = 31
POOL_WINDOWS = (2, 4, 8, 16)
POOL_GROUP_DIM = D_POOL // len(POOL_WINDOWS)
D_IN = 2 * D_CONV + D_POOL
D_FF = 2816
FFN_CONV_WIDTH = 3
EPS = 1e-6

BATCH = 8
LANES = 128
TT = 64
ROWS = TT * BATCH
N_XIN = 3
FF_CHUNK = 256
N_FF_CHUNKS = D_FF // FF_CHUNK
UP_LOOKAHEAD = 2
CONV_STEPS = 8
CONV_TAPS = 8
CONV_ROWS = CONV_STEPS * BATCH
N_CONV_CHUNKS = ROWS // CONV_ROWS
HIST_A = (CONV_WIDTH - 1) * BATCH
HIST_B = (max(POOL_WINDOWS) - 1) * BATCH
HIST_F = (FFN_CONV_WIDTH - 1) * BATCH
CONV_OUT0 = HIST_A + ROWS
CONV_W0 = CONV_OUT0 + ROWS
FFN0 = CONV_W0 + CONV_WIDTH * BATCH
FFN_BLOCK_ROWS = HIST_F + ROWS
WORK_ROWS = FFN0 + N_FF_CHUNKS * FFN_BLOCK_ROWS
VMEM_LIMIT_BYTES = 62 * 1024 * 1024


def _rms_scale(v):
    return v * lax.rsqrt(jnp.mean(v * v, axis=-1, keepdims=True) + EPS)


def _deal(pieces, n_slots):
    base, extra = divmod(len(pieces), n_slots)
    out, i = [], 0
    for s in range(n_slots):
        n = base + (1 if s < extra else 0)
        out.append(pieces[i:i + n])
        i += n
    return out


def _step(g, n_tiles,
          zero_ref, x_hbm, gmix_ref, win_ref, cw_ref, cb_ref, lng_ref, lnb_ref,
          pw_ref, ps_ref, wout_ref, gffn_ref, wup_ref, fw_ref, fb_ref,
          wdown_ref, gfin_ref, o_hbm,
          xin_ref, obuf_ref, h_ref, h2_ref, work_ref, bext_ref,
          act_ref, mix_ref, xmid_ref, sem_in, sem_out):
    slot = lax.rem(g, 2)
    prev = 1 - slot
    xslot = lax.rem(g, N_XIN)

    def in_copy(tile, s, b):
        return pltpu.make_async_copy(x_hbm.at[b, pl.ds(tile * TT, TT), :],
                                     xin_ref.at[s, :, b, :], sem_in.at[s, b])

    def out_copy(tile, s, b):
        return pltpu.make_async_copy(obuf_ref.at[s, :, b, :],
                                     o_hbm.at[b, pl.ds(tile * TT, TT), :], sem_out.at[s, b])

    def out_tile(step):
        return lax.rem(step + n_tiles - 1, n_tiles)

    def x_tile(s):
        return xin_ref[s].reshape(ROWS, D_MODEL)

    def mixer_norm(s):
        h_ref[...] = (_rms_scale(x_tile(s)) * gmix_ref[...]).astype(jnp.bfloat16)

    def gate_rows(c, start, size):
        lo = FFN0 + c * FFN_BLOCK_ROWS + start
        return slice(lo, lo + size)

    @pl.when(g == 0)
    def _():
        for t in range(2):
            for b in range(BATCH):
                in_copy(t, t, b).start()
        for b in range(BATCH):
            in_copy(0, 0, b).wait()
        mixer_norm(0)
        work_ref[0:HIST_A, :] = jnp.zeros((HIST_A, D_CONV), jnp.float32)
        for k in range(CONV_WIDTH):
            work_ref[CONV_W0 + k * BATCH:CONV_W0 + (k + 1) * BATCH, :] = cw_ref[k]
        bext_ref[0:HIST_B, :] = jnp.zeros((HIST_B, D_POOL), jnp.float32)
        for c in range(N_FF_CHUNKS):
            work_ref[gate_rows(c, 0, HIST_F), 0:FF_CHUNK] = jnp.zeros((HIST_F, FF_CHUNK), jnp.float32)
        xmid_ref[1] = jnp.zeros((ROWS, D_MODEL), jnp.float32)

    @pl.when(g >= 2)
    def _():
        for b in range(BATCH):
            out_copy(out_tile(g - 2), prev, b).wait()

    for b in range(BATCH):
        in_copy(jnp.minimum(g + 1, n_tiles - 1), lax.rem(g + 1, N_XIN), b).wait()
    for b in range(BATCH):
        in_copy(jnp.minimum(g + 2, n_tiles - 1), lax.rem(g + 2, N_XIN), b).start()

    def ffn_up(c):
        for half in range(2):
            work_ref[gate_rows(c, HIST_F, ROWS), half * FF_CHUNK:(half + 1) * FF_CHUNK] = jnp.dot(
                h2_ref[...], wup_ref[:, half * D_FF + c * FF_CHUNK:half * D_FF + (c + 1) * FF_CHUNK],
                preferred_element_type=jnp.float32)

    def ffn_act(c):
        cols = slice(c * FF_CHUNK, (c + 1) * FF_CHUNK)
        gate = lambda k: work_ref[gate_rows(c, k * BATCH, ROWS), 0:FF_CHUNK]
        conv = fb_ref[:, cols] + fw_ref[FFN_CONV_WIDTH - 1:FFN_CONV_WIDTH, cols] * gate(FFN_CONV_WIDTH - 1)
        for k in range(FFN_CONV_WIDTH - 1):
            conv = conv + fw_ref[k:k + 1, cols] * gate(k)
        work_ref[gate_rows(c, 0, HIST_F), 0:FF_CHUNK] = work_ref[gate_rows(c, ROWS, HIST_F), 0:FF_CHUNK]
        conv = conv.astype(jnp.bfloat16)
        val = work_ref[gate_rows(c, HIST_F, ROWS), FF_CHUNK:2 * FF_CHUNK].astype(jnp.bfloat16)
        act_ref[:, cols] = conv * jax.nn.sigmoid(conv) * val

    def ffn_down():
        ffn = jnp.dot(act_ref[...], wdown_ref[...], preferred_element_type=jnp.float32)
        out = _rms_scale(xmid_ref[prev] + ffn) * gfin_ref[...]
        obuf_ref[prev] = out.reshape(TT, BATCH, D_MODEL)

    def mixer_head():
        proj = jnp.dot(h_ref[...], win_ref[:, 0:2 * D_CONV], preferred_element_type=jnp.float32)
        work_ref[HIST_A:HIST_A + ROWS, :] = proj[:, 0:D_CONV] * jax.nn.sigmoid(proj[:, D_CONV:2 * D_CONV])

    row_zero = pl.multiple_of(zero_ref[0], BATCH)

    def conv_rows(start):
        return pl.ds(row_zero + start, BATCH)

    def conv_block(ci, lc):
        r0 = ci * CONV_ROWS
        lanes = slice(lc * LANES, (lc + 1) * LANES)
        acc = [jnp.broadcast_to(cb_ref[:, lanes], (BATCH, LANES))] * CONV_STEPS
        for k0 in range(0, CONV_WIDTH, CONV_TAPS):
            taps = range(k0, min(k0 + CONV_TAPS, CONV_WIDTH))
            w = {k: work_ref[conv_rows(CONV_W0 + k * BATCH), lanes] for k in taps}
            u = {j: work_ref[conv_rows(r0 + j * BATCH), lanes]
                 for j in range(taps[0], taps[-1] + CONV_STEPS)}
            for k in taps:
                for t in range(CONV_STEPS):
                    acc[t] = acc[t] + w[k] * u[t + k]
        for t in range(CONV_STEPS):
            out0 = CONV_OUT0 + r0 + t * BATCH
            work_ref[out0:out0 + BATCH, lanes] = acc[t]

    def conv_norm(ci):
        r0 = ci * CONV_ROWS
        acc = work_ref[CONV_OUT0 + r0:CONV_OUT0 + r0 + CONV_ROWS, :]
        mu = jnp.mean(acc, axis=-1, keepdims=True)
        xc = acc - mu
        var = jnp.mean(xc * xc, axis=-1, keepdims=True)
        y = xc * lax.rsqrt(var + EPS) * lng_ref[...] + lnb_ref[...]
        mix_ref[r0:r0 + CONV_ROWS, 0:D_CONV] = (y * jax.nn.sigmoid(y)).astype(jnp.bfloat16)

    def pool_in():
        bext_ref[HIST_B:HIST_B + ROWS, :] = jnp.dot(
            h_ref[...], win_ref[:, 2 * D_CONV:D_IN], preferred_element_type=jnp.float32)

    def pool_group(gi):
        w = POOL_WINDOWS[gi]
        c0 = gi * POOL_GROUP_DIM
        step = lax.broadcasted_iota(jnp.int32, (ROWS, 1), 0) // BATCH
        pos = (g * TT + 1 + step).astype(jnp.float32)
        tok = bext_ref[HIST_B:HIST_B + ROWS, c0:c0 + POOL_GROUP_DIM]
        wsum = tok
        for i in range(1, w):
            lo = HIST_B - i * BATCH
            wsum = wsum + bext_ref[lo:lo + ROWS, c0:c0 + POOL_GROUP_DIM]
        d = wsum / jnp.minimum(pos, float(w)) - tok
        yb = jnp.dot(d.astype(jnp.bfloat16), pw_ref[gi], preferred_element_type=jnp.float32)
        yb = yb * ps_ref[:, c0:c0 + POOL_GROUP_DIM]
        mix_ref[:, D_CONV + c0:D_CONV + c0 + POOL_GROUP_DIM] = yb.astype(jnp.bfloat16)

    def mixer_tail():
        work_ref[0:HIST_A, :] = work_ref[ROWS:ROWS + HIST_A, :]
        bext_ref[0:HIST_B, :] = bext_ref[ROWS:ROWS + HIST_B, :]
        xmid_ref[slot] = x_tile(xslot) + jnp.dot(mix_ref[...], wout_ref[...],
                                                 preferred_element_type=jnp.float32)

    conv_pieces = [functools.partial(piece, ci)
                   for ci in range(N_CONV_CHUNKS)
                   for piece in ([functools.partial(conv_block, lc=lc) for lc in range(D_CONV // LANES)]
                                 + [conv_norm])]
    mixer_pieces = (conv_pieces + [pool_in]
                    + [functools.partial(pool_group, gi) for gi in range(len(POOL_WINDOWS))])
    mixer_slots = _deal(mixer_pieces, N_FF_CHUNKS - 1) + [[]]

    mixer_head()
    h2_ref[...] = (_rms_scale(xmid_ref[prev]) * gffn_ref[...]).astype(jnp.bfloat16)
    for c in range(min(UP_LOOKAHEAD, N_FF_CHUNKS)):
        ffn_up(c)
    for c in range(N_FF_CHUNKS):
        for piece in mixer_slots[c]:
            piece()
        if c + UP_LOOKAHEAD < N_FF_CHUNKS:
            ffn_up(c + UP_LOOKAHEAD)
        ffn_act(c)
    ffn_down()
    mixer_norm(lax.rem(g + 1, N_XIN))
    mixer_tail()
    for b in range(BATCH):
        out_copy(out_tile(g), prev, b).start()

    @pl.when(g == n_tiles)
    def _():
        for b in range(BATCH):
            out_copy(out_tile(g - 1), slot, b).wait()
        for b in range(BATCH):
            out_copy(out_tile(g), prev, b).wait()
        for b in range(BATCH):
            in_copy(n_tiles - 1, lax.rem(g + 2, N_XIN), b).wait()


def _block_kernel(n_tiles, *refs):
    def body(g, carry):
        _step(g, n_tiles, *refs)
        return carry

    lax.fori_loop(0, n_tiles + 1, body, 0)


@jax.jit
def kernel(x, norm_mix_g, w_in, conv_a_w, conv_a_b, ln_a_g, ln_a_b, pool_w, pool_scale, w_out, norm_ffn_g, w_up, conv_f_w, conv_f_b, w_down, norm_final_g):
    batch, seq, d_model = x.shape
    assert batch == BATCH and d_model == D_MODEL and seq % TT == 0
    assert w_in.shape[0] == 1, "one layer"
    bf16 = jnp.bfloat16
    f32 = jnp.float32
    n_tiles = seq // TT
    assert n_tiles >= N_XIN
    row = lambda v: v.reshape(1, -1)

    operands = (
        jnp.zeros((1,), jnp.int32), x, row(norm_mix_g[0]), w_in[0].astype(bf16),
        jnp.broadcast_to(conv_a_w[0][:, None, :], (CONV_WIDTH, BATCH, D_CONV)),
        row(conv_a_b[0]), row(ln_a_g[0]), row(ln_a_b[0]), pool_w[0].astype(bf16), row(pool_scale[0]),
        w_out[0].astype(bf16), row(norm_ffn_g[0]),
        w_up[0].astype(bf16),
        conv_f_w[0], row(conv_f_b[0]), w_down[0].astype(bf16), row(norm_final_g),
    )
    in_specs = ([pl.BlockSpec(memory_space=pltpu.SMEM), pl.BlockSpec(memory_space=pl.ANY)]
                + [pl.BlockSpec(memory_space=pltpu.VMEM)] * len(operands[2:]))

    return pl.pallas_call(
        functools.partial(_block_kernel, n_tiles),
        out_shape=jax.ShapeDtypeStruct(x.shape, x.dtype),
        in_specs=in_specs,
        out_specs=pl.BlockSpec(memory_space=pl.ANY),
        scratch_shapes=[
            pltpu.VMEM((N_XIN, TT, BATCH, D_MODEL), f32),
            pltpu.VMEM((2, TT, BATCH, D_MODEL), f32),
            pltpu.VMEM((ROWS, D_MODEL), bf16),
            pltpu.VMEM((ROWS, D_MODEL), bf16),
            pltpu.VMEM((WORK_ROWS, D_CONV), f32),
            pltpu.VMEM((HIST_B + ROWS, D_POOL), f32),
            pltpu.VMEM((ROWS, D_FF), bf16),
            pltpu.VMEM((ROWS, D_MODEL), bf16),
            pltpu.VMEM((2, ROWS, D_MODEL), f32),
            pltpu.SemaphoreType.DMA((N_XIN, BATCH)),
            pltpu.SemaphoreType.DMA((2, BATCH)),
        ],
        compiler_params=pltpu.CompilerParams(
            vmem_limit_bytes=VMEM_LIMIT_BYTES,
        ),
        name="hybrid_block",
    )(*operands)
```

```python
import functools

import jax
import jax.numpy as jnp
from jax import lax
from jax.experimental import pallas as pl
from jax.experimental.pallas import tpu as pltpu

D_MODEL = 1024
D_CONV = 512
D_POOL = 512
CONV_WIDTH = 31
POOL_WINDOWS = (2, 4, 8, 16)
POOL_GROUP_DIM = D_POOL // len(POOL_WINDOWS)
D_IN = 2 * D_CONV + D_POOL
D_FF = 2816
FFN_CONV_WIDTH = 3
EPS = 1e-6

BATCH = 8
LANES = 128
TT = 64
ROWS = TT * BATCH
N_XIN = 3
FF_CHUNK = 256
N_FF_CHUNKS = D_FF // FF_CHUNK
UP_LOOKAHEAD = 2
CONV_STEPS = 8
CONV_TAPS = 8
CONV_ROWS = CONV_STEPS * BATCH
N_CONV_CHUNKS = ROWS // CONV_ROWS
HIST_A = (CONV_WIDTH - 1) * BATCH
HIST_B = (max(POOL_WINDOWS) - 1) * BATCH
HIST_F = (FFN_CONV_WIDTH - 1) * BATCH
CONV_OUT0 = HIST_A + ROWS
CONV_W0 = CONV_OUT0 + ROWS
FFN0 = CONV_W0 + CONV_WIDTH * BATCH
FFN_BLOCK_ROWS = HIST_F + ROWS
WORK_ROWS = FFN0 + N_FF_CHUNKS * FFN_BLOCK_ROWS
VMEM_LIMIT_BYTES = 62 * 1024 * 1024


def _rms_scale(v):
    return v * lax.rsqrt(jnp.mean(v * v, axis=-1, keepdims=True) + EPS)


def _deal(pieces, n_slots):
    base, extra = divmod(len(pieces), n_slots)
    out, i = [], 0
    for s in range(n_slots):
        n = base + (1 if s < extra else 0)
        out.append(pieces[i:i + n])
        i += n
    return out


def _step(g, n_tiles,
          zero_ref, x_hbm, gmix_ref, win_ref, cw_ref, cb_ref, lng_ref, lnb_ref,
          pw_ref, ps_ref, wout_ref, gffn_ref, wup_ref, fw_ref, fb_ref,
          wdown_ref, gfin_ref, o_hbm,
          xin_ref, obuf_ref, h_ref, h2_ref, work_ref, bext_ref,
          act_ref, mix_ref, xmid_ref, sem_in, sem_out):
    slot = lax.rem(g, 2)
    prev = 1 - slot
    xslot = lax.rem(g, N_XIN)

    def in_copy(tile, s, b):
        return pltpu.make_async_copy(x_hbm.at[b, pl.ds(tile * TT, TT), :],
                                     xin_ref.at[s, :, b, :], sem_in.at[s, b])

    def out_copy(tile, s, b):
        return pltpu.make_async_copy(obuf_ref.at[s, :, b, :],
                                     o_hbm.at[b, pl.ds(tile * TT, TT), :], sem_out.at[s, b])

    def out_tile(step):
        return lax.rem(step + n_tiles - 1, n_tiles)

    def x_tile(s):
        return xin_ref[s].reshape(ROWS, D_MODEL)

    def mixer_norm(s):
        h_ref[...] = (_rms_scale(x_tile(s)) * gmix_ref[...]).astype(jnp.bfloat16)

    def gate_rows(c, start, size):
        lo = FFN0 + c * FFN_BLOCK_ROWS + start
        return slice(lo, lo + size)

    @pl.when(g == 0)
    def _():
        for t in range(2):
            for b in range(BATCH):
                in_copy(t, t, b).start()
        for b in range(BATCH):
            in_copy(0, 0, b).wait()
        mixer_norm(0)
        work_ref[0:HIST_A, :] = jnp.zeros((HIST_A, D_CONV), jnp.float32)
        for k in range(CONV_WIDTH):
            work_ref[CONV_W0 + k * BATCH:CONV_W0 + (k + 1) * BATCH, :] = cw_ref[k]
        bext_ref[0:HIST_B, :] = jnp.zeros((HIST_B, D_POOL), jnp.float32)
        for c in range(N_FF_CHUNKS):
            work_ref[gate_rows(c, 0, HIST_F), 0:FF_CHUNK] = jnp.zeros((HIST_F, FF_CHUNK), jnp.float32)
        xmid_ref[1] = jnp.zeros((ROWS, D_MODEL), jnp.float32)

    @pl.when(g >= 2)
    def _():
        for b in range(BATCH):
            out_copy(out_tile(g - 2), prev, b).wait()

    for b in range(BATCH):
        in_copy(jnp.minimum(g + 1, n_tiles - 1), lax.rem(g + 1, N_XIN), b).wait()
    for b in range(BATCH):
        in_copy(jnp.minimum(g + 2, n_tiles - 1), lax.rem(g + 2, N_XIN), b).start()

    def ffn_up(c):
        for half in range(2):
            work_ref[gate_rows(c, HIST_F, ROWS), half * FF_CHUNK:(half + 1) * FF_CHUNK] = jnp.dot(
                h2_ref[...], wup_ref[:, half * D_FF + c * FF_CHUNK:half * D_FF + (c + 1) * FF_CHUNK],
                preferred_element_type=jnp.float32)

    def ffn_act(c):
        cols = slice(c * FF_CHUNK, (c + 1) * FF_CHUNK)
        gate = lambda k: work_ref[gate_rows(c, k * BATCH, ROWS), 0:FF_CHUNK]
        conv = fb_ref[:, cols] + fw_ref[FFN_CONV_WIDTH - 1:FFN_CONV_WIDTH, cols] * gate(FFN_CONV_WIDTH - 1)
        for k in range(FFN_CONV_WIDTH - 1):
            conv = conv + fw_ref[k:k + 1, cols] * gate(k)
        work_ref[gate_rows(c, 0, HIST_F), 0:FF_CHUNK] = work_ref[gate_rows(c, ROWS, HIST_F), 0:FF_CHUNK]
        conv = conv.astype(jnp.bfloat16)
        val = work_ref[gate_rows(c, HIST_F, ROWS), FF_CHUNK:2 * FF_CHUNK].astype(jnp.bfloat16)
        act_ref[:, cols] = conv * jax.nn.sigmoid(conv) * val

    def ffn_down():
        ffn = jnp.dot(act_ref[...], wdown_ref[...], preferred_element_type=jnp.float32)
        out = _rms_scale(xmid_ref[prev] + ffn) * gfin_ref[...]
        obuf_ref[prev] = out.reshape(TT, BATCH, D_MODEL)

    def mixer_head():
        proj = jnp.dot(h_ref[...], win_ref[:, 0:2 * D_CONV], preferred_element_type=jnp.float32)
        work_ref[HIST_A:HIST_A + ROWS, :] = proj[:, 0:D_CONV] * jax.nn.sigmoid(proj[:, D_CONV:2 * D_CONV])

    row_zero = pl.multiple_of(zero_ref[0], BATCH)

    def conv_rows(start):
        return pl.ds(row_zero + start, BATCH)

    def conv_block(ci, lc):
        r0 = ci * CONV_ROWS
        lanes = slice(lc * LANES, (lc + 1) * LANES)
        acc = [jnp.broadcast_to(cb_ref[:, lanes], (BATCH, LANES))] * CONV_STEPS
        for k0 in range(0, CONV_WIDTH, CONV_TAPS):
            taps = range(k0, min(k0 + CONV_TAPS, CONV_WIDTH))
            w = {k: work_ref[conv_rows(CONV_W0 + k * BATCH), lanes] for k in taps}
            u = {j: work_ref[conv_rows(r0 + j * BATCH), lanes]
                 for j in range(taps[0], taps[-1] + CONV_STEPS)}
            for k in taps:
                for t in range(CONV_STEPS):
                    acc[t] = acc[t] + w[k] * u[t + k]
        for t in range(CONV_STEPS):
            out0 = CONV_OUT0 + r0 + t * BATCH
            work_ref[out0:out0 + BATCH, lanes] = acc[t]

    def conv_norm(ci):
        r0 = ci * CONV_ROWS
        acc = work_ref[CONV_OUT0 + r0:CONV_OUT0 + r0 + CONV_ROWS, :]
        mu = jnp.mean(acc, axis=-1, keepdims=True)
        xc = acc - mu
        var = jnp.mean(xc * xc, axis=-1, keepdims=True)
        y = xc * lax.rsqrt(var + EPS) * lng_ref[...] + lnb_ref[...]
        mix_ref[r0:r0 + CONV_ROWS, 0:D_CONV] = (y * jax.nn.sigmoid(y)).astype(jnp.bfloat16)

    def pool_in():
        bext_ref[HIST_B:HIST_B + ROWS, :] = jnp.dot(
            h_ref[...], win_ref[:, 2 * D_CONV:D_IN], preferred_element_type=jnp.float32)

    def pool_group(gi):
        w = POOL_WINDOWS[gi]
        c0 = gi * POOL_GROUP_DIM
        step = lax.broadcasted_iota(jnp.int32, (ROWS, 1), 0) // BATCH
        pos = (g * TT + 1 + step).astype(jnp.float32)
        tok = bext_ref[HIST_B:HIST_B + ROWS, c0:c0 + POOL_GROUP_DIM]
        wsum = tok
        for i in range(1, w):
            lo = HIST_B - i * BATCH
            wsum = wsum + bext_ref[lo:lo + ROWS, c0:c0 + POOL_GROUP_DIM]
        d = wsum / jnp.minimum(pos, float(w)) - tok
        yb = jnp.dot(d.astype(jnp.bfloat16), pw_ref[gi], preferred_element_type=jnp.float32)
        yb = yb * ps_ref[:, c0:c0 + POOL_GROUP_DIM]
        mix_ref[:, D_CONV + c0:D_CONV + c0 + POOL_GROUP_DIM] = yb.astype(jnp.bfloat16)

    def mixer_tail():
        work_ref[0:HIST_A, :] = work_ref[ROWS:ROWS + HIST_A, :]
        bext_ref[0:HIST_B, :] = bext_ref[ROWS:ROWS + HIST_B, :]
        xmid_ref[slot] = x_tile(xslot) + jnp.dot(mix_ref[...], wout_ref[...],
                                                 preferred_element_type=jnp.float32)

    conv_pieces = [functools.partial(piece, ci)
                   for ci in range(N_CONV_CHUNKS)
                   for piece in ([functools.partial(conv_block, lc=lc) for lc in range(D_CONV // LANES)]
                                 + [conv_norm])]
    pool_pieces = [pool_in] + [functools.partial(pool_group, gi) for gi in range(len(POOL_WINDOWS))]
    mixer_slots = _deal(conv_pieces, N_FF_CHUNKS)

    mixer_head()
    h2_ref[...] = (_rms_scale(xmid_ref[prev]) * gffn_ref[...]).astype(jnp.bfloat16)
    for c in range(min(UP_LOOKAHEAD, N_FF_CHUNKS)):
        ffn_up(c)
    for c in range(N_FF_CHUNKS):
        for piece in mixer_slots[c]:
            piece()
        if c + UP_LOOKAHEAD < N_FF_CHUNKS:
            ffn_up(c + UP_LOOKAHEAD)
        ffn_act(c)
    ffn_down()
    for piece in pool_pieces:
        piece()
    mixer_norm(lax.rem(g + 1, N_XIN))
    mixer_tail()
    for b in range(BATCH):
        out_copy(out_tile(g), prev, b).start()

    @pl.when(g == n_tiles)
    def _():
        for b in range(BATCH):
            out_copy(out_tile(g - 1), slot, b).wait()
        for b in range(BATCH):
            out_copy(out_tile(g), prev, b).wait()
        for b in range(BATCH):
            in_copy(n_tiles - 1, lax.rem(g + 2, N_XIN), b).wait()


def _block_kernel(n_tiles, *refs):
    def body(g, carry):
        _step(g, n_tiles, *refs)
        return carry

    lax.fori_loop(0, n_tiles + 1, body, 0)


@jax.jit
def kernel(x, norm_mix_g, w_in, conv_a_w, conv_a_b, ln_a_g, ln_a_b, pool_w, pool_scale, w_out, norm_ffn_g, w_up, conv_f_w, conv_f_b, w_down, norm_final_g):
    batch, seq, d_model = x.shape
    assert batch == BATCH and d_model == D_MODEL and seq % TT == 0
    assert w_in.shape[0] == 1, "one layer"
    bf16 = jnp.bfloat16
    f32 = jnp.float32
    n_tiles = seq // TT
    assert n_tiles >= N_XIN
    row = lambda v: v.reshape(1, -1)

    operands = (
        jnp.zeros((1,), jnp.int32), x, row(norm_mix_g[0]), w_in[0].astype(bf16),
        jnp.broadcast_to(conv_a_w[0][:, None, :], (CONV_WIDTH, BATCH, D_CONV)),
        row(conv_a_b[0]), row(ln_a_g[0]), row(ln_a_b[0]), pool_w[0].astype(bf16), row(pool_scale[0]),
        w_out[0].astype(bf16), row(norm_ffn_g[0]),
        w_up[0].astype(bf16),
        conv_f_w[0], row(conv_f_b[0]), w_down[0].astype(bf16), row(norm_final_g),
    )
    in_specs = ([pl.BlockSpec(memory_space=pltpu.SMEM), pl.BlockSpec(memory_space=pl.ANY)]
                + [pl.BlockSpec(memory_space=pltpu.VMEM)] * len(operands[2:]))

    return pl.pallas_call(
        functools.partial(_block_kernel, n_tiles),
        out_shape=jax.ShapeDtypeStruct(x.shape, x.dtype),
        in_specs=in_specs,
        out_specs=pl.BlockSpec(memory_space=pl.ANY),
        scratch_shapes=[
            pltpu.VMEM((N_XIN, TT, BATCH, D_MODEL), f32),
            pltpu.VMEM((2, TT, BATCH, D_MODEL), f32),
            pltpu.VMEM((ROWS, D_MODEL), bf16),
            pltpu.VMEM((ROWS, D_MODEL), bf16),
            pltpu.VMEM((WORK_ROWS, D_CONV), f32),
            pltpu.VMEM((HIST_B + ROWS, D_POOL), f32),
            pltpu.VMEM((ROWS, D_FF), bf16),
            pltpu.VMEM((ROWS, D_MODEL), bf16),
            pltpu.VMEM((2, ROWS, D_MODEL), f32),
            pltpu.SemaphoreType.DMA((N_XIN, BATCH)),
            pltpu.SemaphoreType.DMA((2, BATCH)),
        ],
        compiler_params=pltpu.CompilerParams(
            vmem_limit_bytes=VMEM_LIMIT_BYTES,
        ),
        name="hybrid_block",
    )(*operands)
```

```python
import functools

import jax
import jax.numpy as jnp
from jax import lax
from jax.experimental import pallas as pl
from jax.experimental.pallas import tpu as pltpu

D_MODEL = 1024
D_CONV = 512
D_POOL = 512
CONV_WIDTH = 31
POOL_WINDOWS = (2, 4, 8, 16)
POOL_GROUP_DIM = D_POOL // len(POOL_WINDOWS)
D_IN = 2 * D_CONV + D_POOL
D_FF = 2816
FFN_CONV_WIDTH = 3
EPS = 1e-6

BATCH = 8
LANES = 128
TT = 64
ROWS = TT * BATCH
N_XIN = 3
FF_CHUNK = 256
N_FF_CHUNKS = D_FF // FF_CHUNK
UP_LOOKAHEAD = 2
CONV_STEPS = 8
CONV_TAPS = 8
CONV_ROWS = CONV_STEPS * BATCH
N_CONV_CHUNKS = ROWS // CONV_ROWS
HIST_A = (CONV_WIDTH - 1) * BATCH
HIST_B = (max(POOL_WINDOWS) - 1) * BATCH
HIST_F = (FFN_CONV_WIDTH - 1) * BATCH
CONV_OUT0 = HIST_A + ROWS
CONV_W0 = CONV_OUT0 + ROWS
FFN0 = CONV_W0 + CONV_WIDTH * BATCH
FFN_BLOCK_ROWS = HIST_F + ROWS
WORK_ROWS = FFN0 + N_FF_CHUNKS * FFN_BLOCK_ROWS
VMEM_LIMIT_BYTES = 62 * 1024 * 1024


def _rms_scale(v):
    return v * lax.rsqrt(jnp.mean(v * v, axis=-1, keepdims=True) + EPS)


def _deal(pieces, n_slots):
    base, extra = divmod(len(pieces), n_slots)
    out, i = [], 0
    for s in range(n_slots):
        n = base + (1 if s < extra else 0)
        out.append(pieces[i:i + n])
        i += n
    return out


def _step(g, n_tiles,
          zero_ref, x_hbm, gmix_ref, win_ref, cw_ref, cb_ref, lng_ref, lnb_ref,
          pw_ref, ps_ref, wout_ref, gffn_ref, wup_ref, fw_ref, fb_ref,
          wdown_ref, gfin_ref, o_hbm,
          xin_ref, obuf_ref, h_ref, h2_ref, work_ref, bext_ref,
          act_ref, mix_ref, xmid_ref, sem_in, sem_out):
    slot = lax.rem(g, 2)
    prev = 1 - slot
    xslot = lax.rem(g, N_XIN)

    def in_copy(tile, s, b):
        return pltpu.make_async_copy(x_hbm.at[b, pl.ds(tile * TT, TT), :],
                                     xin_ref.at[s, :, b, :], sem_in.at[s, b])

    def out_copy(tile, s, b):
        return pltpu.make_async_copy(obuf_ref.at[s, :, b, :],
                                     o_hbm.at[b, pl.ds(tile * TT, TT), :], sem_out.at[s, b])

    def out_tile(step):
        return lax.rem(step + n_tiles - 1, n_tiles)

    def x_tile(s):
        return xin_ref[s].reshape(ROWS, D_MODEL)

    def mixer_norm(s):
        h_ref[...] = (_rms_scale(x_tile(s)) * gmix_ref[...]).astype(jnp.bfloat16)

    def gate_rows(c, start, size):
        lo = FFN0 + c * FFN_BLOCK_ROWS + start
        return slice(lo, lo + size)

    @pl.when(g == 0)
    def _():
        for t in range(2):
            for b in range(BATCH):
                in_copy(t, t, b).start()
        for b in range(BATCH):
            in_copy(0, 0, b).wait()
        mixer_norm(0)
        work_ref[0:HIST_A, :] = jnp.zeros((HIST_A, D_CONV), jnp.float32)
        for k in range(CONV_WIDTH):
            work_ref[CONV_W0 + k * BATCH:CONV_W0 + (k + 1) * BATCH, :] = cw_ref[k]
        bext_ref[0:HIST_B, :] = jnp.zeros((HIST_B, D_POOL), jnp.float32)
        for c in range(N_FF_CHUNKS):
            work_ref[gate_rows(c, 0, HIST_F), 0:FF_CHUNK] = jnp.zeros((HIST_F, FF_CHUNK), jnp.float32)
        xmid_ref[1] = jnp.zeros((ROWS, D_MODEL), jnp.float32)

    @pl.when(g >= 2)
    def _():
        for b in range(BATCH):
            out_copy(out_tile(g - 2), prev, b).wait()

    for b in range(BATCH):
        in_copy(jnp.minimum(g + 1, n_tiles - 1), lax.rem(g + 1, N_XIN), b).wait()
    for b in range(BATCH):
        in_copy(jnp.minimum(g + 2, n_tiles - 1), lax.rem(g + 2, N_XIN), b).start()

    def ffn_up(c):
        for half in range(2):
            work_ref[gate_rows(c, HIST_F, ROWS), half * FF_CHUNK:(half + 1) * FF_CHUNK] = jnp.dot(
                h2_ref[...], wup_ref[:, half * D_FF + c * FF_CHUNK:half * D_FF + (c + 1) * FF_CHUNK],
                preferred_element_type=jnp.float32)

    def ffn_act(c):
        cols = slice(c * FF_CHUNK, (c + 1) * FF_CHUNK)
        gate = lambda k: work_ref[gate_rows(c, k * BATCH, ROWS), 0:FF_CHUNK]
        conv = fb_ref[:, cols] + fw_ref[FFN_CONV_WIDTH - 1:FFN_CONV_WIDTH, cols] * gate(FFN_CONV_WIDTH - 1)
        for k in range(FFN_CONV_WIDTH - 1):
            conv = conv + fw_ref[k:k + 1, cols] * gate(k)
        work_ref[gate_rows(c, 0, HIST_F), 0:FF_CHUNK] = work_ref[gate_rows(c, ROWS, HIST_F), 0:FF_CHUNK]
        conv = conv.astype(jnp.bfloat16)
        val = work_ref[gate_rows(c, HIST_F, ROWS), FF_CHUNK:2 * FF_CHUNK].astype(jnp.bfloat16)
        act_ref[:, cols] = conv * jax.nn.sigmoid(conv) * val

    def ffn_down():
        ffn = jnp.dot(act_ref[...], wdown_ref[...], preferred_element_type=jnp.float32)
        out = _rms_scale(xmid_ref[prev] + ffn) * gfin_ref[...]
        obuf_ref[prev] = out.reshape(TT, BATCH, D_MODEL)

    def mixer_head():
        proj = jnp.dot(h_ref[...], win_ref[:, 0:2 * D_CONV], preferred_element_type=jnp.float32)
        work_ref[HIST_A:HIST_A + ROWS, :] = proj[:, 0:D_CONV] * jax.nn.sigmoid(proj[:, D_CONV:2 * D_CONV])

    row_zero = pl.multiple_of(zero_ref[0], BATCH)

    def conv_rows(start):
        return pl.ds(row_zero + start, BATCH)

    def conv_block(ci, lc):
        r0 = ci * CONV_ROWS
        lanes = slice(lc * LANES, (lc + 1) * LANES)
        acc = [jnp.broadcast_to(cb_ref[:, lanes], (BATCH, LANES))] * CONV_STEPS
        for k0 in range(0, CONV_WIDTH, CONV_TAPS):
            taps = range(k0, min(k0 + CONV_TAPS, CONV_WIDTH))
            w = {k: work_ref[conv_rows(CONV_W0 + k * BATCH), lanes] for k in taps}
            u = {j: work_ref[conv_rows(r0 + j * BATCH), lanes]
                 for j in range(taps[0], taps[-1] + CONV_STEPS)}
            for k in taps:
                for t in range(CONV_STEPS):
                    acc[t] = acc[t] + w[k] * u[t + k]
        for t in range(CONV_STEPS):
            out0 = CONV_OUT0 + r0 + t * BATCH
            work_ref[out0:out0 + BATCH, lanes] = acc[t]

    def conv_norm(ci):
        r0 = ci * CONV_ROWS
        acc = work_ref[CONV_OUT0 + r0:CONV_OUT0 + r0 + CONV_ROWS, :]
        mu = jnp.mean(acc, axis=-1, keepdims=True)
        xc = acc - mu
        var = jnp.mean(xc * xc, axis=-1, keepdims=True)
        y = xc * lax.rsqrt(var + EPS) * lng_ref[...] + lnb_ref[...]
        mix_ref[r0:r0 + CONV_ROWS, 0:D_CONV] = (y * jax.nn.sigmoid(y)).astype(jnp.bfloat16)

    def pool_in():
        bext_ref[HIST_B:HIST_B + ROWS, :] = jnp.dot(
            h_ref[...], win_ref[:, 2 * D_CONV:D_IN], preferred_element_type=jnp.float32)

    def pool_group(gi):
        w = POOL_WINDOWS[gi]
        c0 = gi * POOL_GROUP_DIM
        step = lax.broadcasted_iota(jnp.int32, (ROWS, 1), 0) // BATCH
        pos = (g * TT + 1 + step).astype(jnp.float32)
        tok = bext_ref[HIST_B:HIST_B + ROWS, c0:c0 + POOL_GROUP_DIM]
        wsum = tok
        for i in range(1, w):
            lo = HIST_B - i * BATCH
            wsum = wsum + bext_ref[lo:lo + ROWS, c0:c0 + POOL_GROUP_DIM]
        d = wsum / jnp.minimum(pos, float(w)) - tok
        yb = jnp.dot(d.astype(jnp.bfloat16), pw_ref[gi], preferred_element_type=jnp.float32)
        yb = yb * ps_ref[:, c0:c0 + POOL_GROUP_DIM]
        mix_ref[:, D_CONV + c0:D_CONV + c0 + POOL_GROUP_DIM] = yb.astype(jnp.bfloat16)

    def mixer_tail():
        work_ref[0:HIST_A, :] = work_ref[ROWS:ROWS + HIST_A, :]
        bext_ref[0:HIST_B, :] = bext_ref[ROWS:ROWS + HIST_B, :]
        xmid_ref[slot] = x_tile(xslot) + jnp.dot(mix_ref[...], wout_ref[...],
                                                 preferred_element_type=jnp.float32)

    conv_pieces = [functools.partial(piece, ci)
                   for ci in range(N_CONV_CHUNKS)
                   for piece in ([functools.partial(conv_block, lc=lc) for lc in range(D_CONV // LANES)]
                                 + [conv_norm])]
    mixer_pieces = (conv_pieces + [pool_in]
                    + [functools.partial(pool_group, gi) for gi in range(len(POOL_WINDOWS))])
    mixer_slots = _deal(mixer_pieces, N_FF_CHUNKS)

    mixer_head()
    h2_ref[...] = (_rms_scale(xmid_ref[prev]) * gffn_ref[...]).astype(jnp.bfloat16)
    for c in range(min(UP_LOOKAHEAD, N_FF_CHUNKS)):
        ffn_up(c)
    for c in range(N_FF_CHUNKS):
        for piece in mixer_slots[c]:
            piece()
        if c + UP_LOOKAHEAD < N_FF_CHUNKS:
            ffn_up(c + UP_LOOKAHEAD)
        ffn_act(c)
    ffn_down()
    mixer_norm(lax.rem(g + 1, N_XIN))
    mixer_tail()
    for b in range(BATCH):
        out_copy(out_tile(g), prev, b).start()

    @pl.when(g == n_tiles)
    def _():
        for b in range(BATCH):
            out_copy(out_tile(g - 1), slot, b).wait()
        for b in range(BATCH):
            out_copy(out_tile(g), prev, b).wait()
        for b in range(BATCH):
            in_copy(n_tiles - 1, lax.rem(g + 2, N_XIN), b).wait()


def _block_kernel(n_tiles, *refs):
    def body(g, carry):
        _step(g, n_tiles, *refs)
        return carry

    lax.fori_loop(0, n_tiles + 1, body, 0)


@jax.jit
def kernel(x, norm_mix_g, w_in, conv_a_w, conv_a_b, ln_a_g, ln_a_b, pool_w, pool_scale, w_out, norm_ffn_g, w_up, conv_f_w, conv_f_b, w_down, norm_final_g):
    batch, seq, d_model = x.shape
    assert batch == BATCH and d_model == D_MODEL and seq % TT == 0
    assert w_in.shape[0] == 1, "one layer"
    bf16 = jnp.bfloat16
    f32 = jnp.float32
    n_tiles = seq // TT
    assert n_tiles >= N_XIN
    row = lambda v: v.reshape(1, -1)

    operands = (
        jnp.zeros((1,), jnp.int32), x, row(norm_mix_g[0]), w_in[0].astype(bf16),
        jnp.broadcast_to(conv_a_w[0][:, None, :], (CONV_WIDTH, BATCH, D_CONV)),
        row(conv_a_b[0]), row(ln_a_g[0]), row(ln_a_b[0]), pool_w[0].astype(bf16), row(pool_scale[0]),
        w_out[0].astype(bf16), row(norm_ffn_g[0]),
        w_up[0].astype(bf16),
        conv_f_w[0], row(conv_f_b[0]), w_down[0].astype(bf16), row(norm_final_g),
    )
    in_specs = ([pl.BlockSpec(memory_space=pltpu.SMEM), pl.BlockSpec(memory_space=pl.ANY)]
                + [pl.BlockSpec(memory_space=pltpu.VMEM)] * len(operands[2:]))

    return pl.pallas_call(
        functools.partial(_block_kernel, n_tiles),
        out_shape=jax.ShapeDtypeStruct(x.shape, x.dtype),
        in_specs=in_specs,
        out_specs=pl.BlockSpec(memory_space=pl.ANY),
        scratch_shapes=[
            pltpu.VMEM((N_XIN, TT, BATCH, D_MODEL), f32),
            pltpu.VMEM((2, TT, BATCH, D_MODEL), f32),
            pltpu.VMEM((ROWS, D_MODEL), bf16),
            pltpu.VMEM((ROWS, D_MODEL), bf16),
            pltpu.VMEM((WORK_ROWS, D_CONV), f32),
            pltpu.VMEM((HIST_B + ROWS, D_POOL), f32),
            pltpu.VMEM((ROWS, D_FF), bf16),
            pltpu.VMEM((ROWS, D_MODEL), bf16),
            pltpu.VMEM((2, ROWS, D_MODEL), f32),
            pltpu.SemaphoreType.DMA((N_XIN, BATCH)),
            pltpu.SemaphoreType.DMA((2, BATCH)),
        ],
        compiler_params=pltpu.CompilerParams(
            vmem_limit_bytes=VMEM_LIMIT_BYTES,
        ),
        name="hybrid_block",
    )(*operands)
```

```python
import functools

import jax
import jax.numpy as jnp
from jax import lax
from jax.experimental import pallas as pl
from jax.experimental.pallas import tpu as pltpu

D_MODEL = 1024
D_CONV = 512
D_POOL = 512
CONV_WIDTH = 31
POOL_WINDOWS = (2, 4, 8, 16)
POOL_GROUP_DIM = D_POOL // len(POOL_WINDOWS)
D_IN = 2 * D_CONV + D_POOL
D_FF = 2816
FFN_CONV_WIDTH = 3
EPS = 1e-6

BATCH = 8
LANES = 128
TT = 64
ROWS = TT * BATCH
N_XIN = 3
FF_CHUNK = 256
N_FF_CHUNKS = D_FF // FF_CHUNK
UP_LOOKAHEAD = 3
CONV_STEPS = 8
CONV_TAPS = 8
CONV_ROWS = CONV_STEPS * BATCH
N_CONV_CHUNKS = ROWS // CONV_ROWS
HIST_A = (CONV_WIDTH - 1) * BATCH
HIST_B = (max(POOL_WINDOWS) - 1) * BATCH
HIST_F = (FFN_CONV_WIDTH - 1) * BATCH
CONV_OUT0 = HIST_A + ROWS
CONV_W0 = CONV_OUT0 + ROWS
FFN0 = CONV_W0 + CONV_WIDTH * BATCH
FFN_BLOCK_ROWS = HIST_F + ROWS
WORK_ROWS = FFN0 + N_FF_CHUNKS * FFN_BLOCK_ROWS
VMEM_LIMIT_BYTES = 62 * 1024 * 1024


def _rms_scale(v):
    return v * lax.rsqrt(jnp.mean(v * v, axis=-1, keepdims=True) + EPS)


def _deal(pieces, n_slots):
    base, extra = divmod(len(pieces), n_slots)
    out, i = [], 0
    for s in range(n_slots):
        n = base + (1 if s < extra else 0)
        out.append(pieces[i:i + n])
        i += n
    return out


def _step(g, n_tiles,
          zero_ref, x_hbm, gmix_ref, win_ref, cw_ref, cb_ref, lng_ref, lnb_ref,
          pw_ref, ps_ref, wout_ref, gffn_ref, wup_ref, fw_ref, fb_ref,
          wdown_ref, gfin_ref, o_hbm,
          xin_ref, obuf_ref, h_ref, h2_ref, work_ref, bext_ref,
          act_ref, mix_ref, xmid_ref, sem_in, sem_out):
    slot = lax.rem(g, 2)
    prev = 1 - slot
    xslot = lax.rem(g, N_XIN)

    def in_copy(tile, s, b):
        return pltpu.make_async_copy(x_hbm.at[b, pl.ds(tile * TT, TT), :],
                                     xin_ref.at[s, :, b, :], sem_in.at[s, b])

    def out_copy(tile, s, b):
        return pltpu.make_async_copy(obuf_ref.at[s, :, b, :],
                                     o_hbm.at[b, pl.ds(tile * TT, TT), :], sem_out.at[s, b])

    def out_tile(step):
        return lax.rem(step + n_tiles - 1, n_tiles)

    def x_tile(s):
        return xin_ref[s].reshape(ROWS, D_MODEL)

    def mixer_norm(s):
        h_ref[...] = (_rms_scale(x_tile(s)) * gmix_ref[...]).astype(jnp.bfloat16)

    def gate_rows(c, start, size):
        lo = FFN0 + c * FFN_BLOCK_ROWS + start
        return slice(lo, lo + size)

    @pl.when(g == 0)
    def _():
        for t in range(2):
            for b in range(BATCH):
                in_copy(t, t, b).start()
        for b in range(BATCH):
            in_copy(0, 0, b).wait()
        mixer_norm(0)
        work_ref[0:HIST_A, :] = jnp.zeros((HIST_A, D_CONV), jnp.float32)
        for k in range(CONV_WIDTH):
            work_ref[CONV_W0 + k * BATCH:CONV_W0 + (k + 1) * BATCH, :] = cw_ref[k]
        bext_ref[0:HIST_B, :] = jnp.zeros((HIST_B, D_POOL), jnp.float32)
        for c in range(N_FF_CHUNKS):
            work_ref[gate_rows(c, 0, HIST_F), 0:FF_CHUNK] = jnp.zeros((HIST_F, FF_CHUNK), jnp.float32)
        xmid_ref[1] = jnp.zeros((ROWS, D_MODEL), jnp.float32)

    @pl.when(g >= 2)
    def _():
        for b in range(BATCH):
            out_copy(out_tile(g - 2), prev, b).wait()

    for b in range(BATCH):
        in_copy(jnp.minimum(g + 1, n_tiles - 1), lax.rem(g + 1, N_XIN), b).wait()
    for b in range(BATCH):
        in_copy(jnp.minimum(g + 2, n_tiles - 1), lax.rem(g + 2, N_XIN), b).start()

    def ffn_up(c):
        for half in range(2):
            work_ref[gate_rows(c, HIST_F, ROWS), half * FF_CHUNK:(half + 1) * FF_CHUNK] = jnp.dot(
                h2_ref[...], wup_ref[:, half * D_FF + c * FF_CHUNK:half * D_FF + (c + 1) * FF_CHUNK],
                preferred_element_type=jnp.float32)

    def ffn_act(c):
        cols = slice(c * FF_CHUNK, (c + 1) * FF_CHUNK)
        gate = lambda k: work_ref[gate_rows(c, k * BATCH, ROWS), 0:FF_CHUNK]
        conv = fb_ref[:, cols] + fw_ref[FFN_CONV_WIDTH - 1:FFN_CONV_WIDTH, cols] * gate(FFN_CONV_WIDTH - 1)
        for k in range(FFN_CONV_WIDTH - 1):
            conv = conv + fw_ref[k:k + 1, cols] * gate(k)
        work_ref[gate_rows(c, 0, HIST_F), 0:FF_CHUNK] = work_ref[gate_rows(c, ROWS, HIST_F), 0:FF_CHUNK]
        conv = conv.astype(jnp.bfloat16)
        val = work_ref[gate_rows(c, HIST_F, ROWS), FF_CHUNK:2 * FF_CHUNK].astype(jnp.bfloat16)
        act_ref[:, cols] = conv * jax.nn.sigmoid(conv) * val

    def ffn_down():
        ffn = jnp.dot(act_ref[...], wdown_ref[...], preferred_element_type=jnp.float32)
        out = _rms_scale(xmid_ref[prev] + ffn) * gfin_ref[...]
        obuf_ref[prev] = out.reshape(TT, BATCH, D_MODEL)

    def mixer_head():
        proj = jnp.dot(h_ref[...], win_ref[:, 0:2 * D_CONV], preferred_element_type=jnp.float32)
        work_ref[HIST_A:HIST_A + ROWS, :] = proj[:, 0:D_CONV] * jax.nn.sigmoid(proj[:, D_CONV:2 * D_CONV])

    row_zero = pl.multiple_of(zero_ref[0], BATCH)

    def conv_rows(start):
        return pl.ds(row_zero + start, BATCH)

    def conv_block(ci, lc):
        r0 = ci * CONV_ROWS
        lanes = slice(lc * LANES, (lc + 1) * LANES)
        acc = [jnp.broadcast_to(cb_ref[:, lanes], (BATCH, LANES))] * CONV_STEPS
        for k0 in range(0, CONV_WIDTH, CONV_TAPS):
            taps = range(k0, min(k0 + CONV_TAPS, CONV_WIDTH))
            w = {k: work_ref[conv_rows(CONV_W0 + k * BATCH), lanes] for k in taps}
            u = {j: work_ref[conv_rows(r0 + j * BATCH), lanes]
                 for j in range(taps[0], taps[-1] + CONV_STEPS)}
            for k in taps:
                for t in range(CONV_STEPS):
                    acc[t] = acc[t] + w[k] * u[t + k]
        for t in range(CONV_STEPS):
            out0 = CONV_OUT0 + r0 + t * BATCH
            work_ref[out0:out0 + BATCH, lanes] = acc[t]

    def conv_norm(ci):
        r0 = ci * CONV_ROWS
        acc = work_ref[CONV_OUT0 + r0:CONV_OUT0 + r0 + CONV_ROWS, :]
        mu = jnp.mean(acc, axis=-1, keepdims=True)
        xc = acc - mu
        var = jnp.mean(xc * xc, axis=-1, keepdims=True)
        y = xc * lax.rsqrt(var + EPS) * lng_ref[...] + lnb_ref[...]
        mix_ref[r0:r0 + CONV_ROWS, 0:D_CONV] = (y * jax.nn.sigmoid(y)).astype(jnp.bfloat16)

    def pool_in():
        bext_ref[HIST_B:HIST_B + ROWS, :] = jnp.dot(
            h_ref[...], win_ref[:, 2 * D_CONV:D_IN], preferred_element_type=jnp.float32)

    def pool_group(gi):
        w = POOL_WINDOWS[gi]
        c0 = gi * POOL_GROUP_DIM
        step = lax.broadcasted_iota(jnp.int32, (ROWS, 1), 0) // BATCH
        pos = (g * TT + 1 + step).astype(jnp.float32)
        tok = bext_ref[HIST_B:HIST_B + ROWS, c0:c0 + POOL_GROUP_DIM]
        wsum = tok
        for i in range(1, w):
            lo = HIST_B - i * BATCH
            wsum = wsum + bext_ref[lo:lo + ROWS, c0:c0 + POOL_GROUP_DIM]
        d = wsum / jnp.minimum(pos, float(w)) - tok
        yb = jnp.dot(d.astype(jnp.bfloat16), pw_ref[gi], preferred_element_type=jnp.float32)
        yb = yb * ps_ref[:, c0:c0 + POOL_GROUP_DIM]
        mix_ref[:, D_CONV + c0:D_CONV + c0 + POOL_GROUP_DIM] = yb.astype(jnp.bfloat16)

    def mixer_tail():
        work_ref[0:HIST_A, :] = work_ref[ROWS:ROWS + HIST_A, :]
        bext_ref[0:HIST_B, :] = bext_ref[ROWS:ROWS + HIST_B, :]
        xmid_ref[slot] = x_tile(xslot) + jnp.dot(mix_ref[...], wout_ref[...],
                                                 preferred_element_type=jnp.float32)

    conv_pieces = [functools.partial(piece, ci)
                   for ci in range(N_CONV_CHUNKS)
                   for piece in ([functools.partial(conv_block, lc=lc) for lc in range(D_CONV // LANES)]
                                 + [conv_norm])]
    mixer_pieces = (conv_pieces + [pool_in]
                    + [functools.partial(pool_group, gi) for gi in range(len(POOL_WINDOWS))])
    mixer_slots = _deal(mixer_pieces, N_FF_CHUNKS)

    mixer_head()
    h2_ref[...] = (_rms_scale(xmid_ref[prev]) * gffn_ref[...]).astype(jnp.bfloat16)
    for c in range(min(UP_LOOKAHEAD, N_FF_CHUNKS)):
        ffn_up(c)
    for c in range(N_FF_CHUNKS):
        for piece in mixer_slots[c]:
            piece()
        if c + UP_LOOKAHEAD < N_FF_CHUNKS:
            ffn_up(c + UP_LOOKAHEAD)
        ffn_act(c)
    ffn_down()
    mixer_norm(lax.rem(g + 1, N_XIN))
    mixer_tail()
    for b in range(BATCH):
        out_copy(out_tile(g), prev, b).start()

    @pl.when(g == n_tiles)
    def _():
        for b in range(BATCH):
            out_copy(out_tile(g - 1), slot, b).wait()
        for b in range(BATCH):
            out_copy(out_tile(g), prev, b).wait()
        for b in range(BATCH):
            in_copy(n_tiles - 1, lax.rem(g + 2, N_XIN), b).wait()


def _block_kernel(n_tiles, *refs):
    def body(g, carry):
        _step(g, n_tiles, *refs)
        return carry

    lax.fori_loop(0, n_tiles + 1, body, 0)


@jax.jit
def kernel(x, norm_mix_g, w_in, conv_a_w, conv_a_b, ln_a_g, ln_a_b, pool_w, pool_scale, w_out, norm_ffn_g, w_up, conv_f_w, conv_f_b, w_down, norm_final_g):
    batch, seq, d_model = x.shape
    assert batch == BATCH and d_model == D_MODEL and seq % TT == 0
    assert w_in.shape[0] == 1, "one layer"
    bf16 = jnp.bfloat16
    f32 = jnp.float32
    n_tiles = seq // TT
    assert n_tiles >= N_XIN
    row = lambda v: v.reshape(1, -1)

    operands = (
        jnp.zeros((1,), jnp.int32), x, row(norm_mix_g[0]), w_in[0].astype(bf16),
        jnp.broadcast_to(conv_a_w[0][:, None, :], (CONV_WIDTH, BATCH, D_CONV)),
        row(conv_a_b[0]), row(ln_a_g[0]), row(ln_a_b[0]), pool_w[0].astype(bf16), row(pool_scale[0]),
        w_out[0].astype(bf16), row(norm_ffn_g[0]),
        w_up[0].astype(bf16),
        conv_f_w[0], row(conv_f_b[0]), w_down[0].astype(bf16), row(norm_final_g),
    )
    in_specs = ([pl.BlockSpec(memory_space=pltpu.SMEM), pl.BlockSpec(memory_space=pl.ANY)]
                + [pl.BlockSpec(memory_space=pltpu.VMEM)] * len(operands[2:]))

    return pl.pallas_call(
        functools.partial(_block_kernel, n_tiles),
        out_shape=jax.ShapeDtypeStruct(x.shape, x.dtype),
        in_specs=in_specs,
        out_specs=pl.BlockSpec(memory_space=pl.ANY),
        scratch_shapes=[
            pltpu.VMEM((N_XIN, TT, BATCH, D_MODEL), f32),
            pltpu.VMEM((2, TT, BATCH, D_MODEL), f32),
            pltpu.VMEM((ROWS, D_MODEL), bf16),
            pltpu.VMEM((ROWS, D_MODEL), bf16),
            pltpu.VMEM((WORK_ROWS, D_CONV), f32),
            pltpu.VMEM((HIST_B + ROWS, D_POOL), f32),
            pltpu.VMEM((ROWS, D_FF), bf16),
            pltpu.VMEM((ROWS, D_MODEL), bf16),
            pltpu.VMEM((2, ROWS, D_MODEL), f32),
            pltpu.SemaphoreType.DMA((N_XIN, BATCH)),
            pltpu.SemaphoreType.DMA((2, BATCH)),
        ],
        compiler_params=pltpu.CompilerParams(
            vmem_limit_bytes=VMEM_LIMIT_BYTES,
        ),
        name="hybrid_block",
    )(*operands)
```
